```python
import math
import jax
import jax.numpy as jnp
from jax import lax
import numpy as np

D_MODEL = 1024
BATCH = 8
SEQ = 2048
DEPTH = 2
DEC_BATCH = 16
DEC_SEQ = 32
PAST_LEN = 1024

CHUNK = 64
Q_BLOCK = 128
EPS = 1e-6
ML_HEADS = 4
ML_DH = 3 * D_MODEL // (8 * ML_HEADS)
ML_W = ML_HEADS * ML_DH
SB_DH = 64
SB_HEADS = 3 * D_MODEL // (8 * SB_DH)
SB_W = SB_HEADS * SB_DH
SSM_W = D_MODEL // 4
SSM_CH = 16
SSM_GROUPS = SSM_W // SSM_CH
SSM_P = 64
MIX_W = ML_W + SB_W + SSM_W
SPLIT_SIZES = (ML_W, ML_W, ML_W, ML_W, ML_HEADS, ML_HEADS, SB_W, SB_W, SB_W, SSM_W)
N_IN = 4 * ML_W + 2 * ML_HEADS + 3 * SB_W + SSM_W
N_MEM = 256
X_HEADS = 4
X_DH = D_MODEL // X_HEADS
D_FF = 128 * math.ceil(8 * D_MODEL / (3 * 128))
CONV_W = 3

kernel_name = 'hybrid_mlstm_stickbreak_s5_stream_step'


def head_rmsnorm(x, g, n_heads):
    sh = x.shape
    xh = x.astype(jnp.float32).reshape(sh[:-1] + (n_heads, sh[-1] // n_heads))
    y = xh * lax.rsqrt(jnp.mean(xh * xh, axis=-1, keepdims=True) + EPS)
    return (y.reshape(sh) * g.astype(jnp.float32)).astype(x.dtype)


def rmsnorm(x, g):
    return head_rmsnorm(x, g, 1)


def split_cols(z):
    out, off = [], 0
    for w in SPLIT_SIZES:
        out.append(z[..., off:off + w])
        off += w
    return out


def mlstm(q, k, v, i_pre, f_pre, c0, n0, m0):
    f32 = jnp.float32
    bsz, t_len, nh, d = q.shape
    blk = min(CHUNK, t_len)
    nc = t_len // blk
    q = q.astype(f32)
    k = k.astype(f32) * (d ** -0.5)
    v = v.astype(f32)
    logi = i_pre.astype(f32)
    logf = jax.nn.log_sigmoid(f_pre.astype(f32))

    def to_chunks(a):
        a = a.reshape((bsz, nc, blk) + a.shape[2:])
        return jnp.swapaxes(jnp.moveaxis(a, 1, 0), 2, 3)

    xs = (to_chunks(q), to_chunks(k), to_chunks(v), to_chunks(logi), to_chunks(logf))
    causal = jnp.tril(jnp.ones((blk, blk), dtype=bool))

    def step(carry, xc):
        c, n, m = carry
        qc, kc, vc, li, lf = xc
        b = jnp.cumsum(lf, axis=-1)
        dmat = b[..., :, None] - b[..., None, :] + li[..., None, :]
        dmat = jnp.where(causal, dmat, -jnp.inf)
        inter = b + m[..., None]
        m_t = jnp.maximum(inter, jnp.max(dmat, axis=-1))
        w_intra = jnp.exp(dmat - m_t[..., None])
        w_inter = jnp.exp(inter - m_t)
        s = jnp.einsum('bhtd,bhsd->bhts', qc, kc) * w_intra
        num = jnp.einsum('bhts,bhsd->bhtd', s, vc) + w_inter[..., None] * jnp.einsum('bhvk,bhtk->bhtv', c, qc)
        den = jnp.sum(s, axis=-1) + w_inter * jnp.einsum('bhk,bhtk->bht', n, qc)
        h = num / jnp.maximum(jnp.abs(den), jnp.exp(-m_t))[..., None]
        wg = w_intra[..., -1, :]
        decay = w_inter[..., -1]
        c_new = decay[..., None, None] * c + jnp.einsum('bhsv,bhsk->bhvk', vc * wg[..., None], kc)
        n_new = decay[..., None] * n + jnp.einsum('bhs,bhsk->bhk', wg, kc)
        return (c_new, n_new, m_t[..., -1]), h

    (c, n, m), hs = lax.scan(step, (c0.astype(f32), n0.astype(f32), m0.astype(f32)), xs)
    hs = jnp.transpose(hs, (1, 0, 3, 2, 4)).reshape(bsz, t_len, nh * d)
    return hs, c, n, m


def stick_breaking_block(q, k, v, q_pos, k_pos):
    f32 = jnp.float32
    z = jnp.einsum('bthd,bshd->bhts', q.astype(f32), k.astype(f32)) * (q.shape[-1] ** -0.5)
    vis = k_pos[None, :] < q_pos[:, None]
    log_1m = jnp.where(vis, jax.nn.log_sigmoid(-z), 0.0)
    after = lax.cumsum(log_1m, axis=3, reverse=True) - log_1m
    a = jnp.where(vis, jnp.exp(jax.nn.log_sigmoid(z) + after), 0.0)
    return jnp.einsum('bhts,bshd->bthd', a, v.astype(f32))


def stick_breaking_prompt(q, k, v):
    t_len = q.shape[1]
    pos = jnp.arange(t_len)
    outs = []
    for qb in range(t_len // Q_BLOCK):
        lo, hi = qb * Q_BLOCK, (qb + 1) * Q_BLOCK
        outs.append(stick_breaking_block(q[:, lo:hi], k[:, :hi], v[:, :hi], pos[lo:hi], pos[:hi]))
    return jnp.concatenate(outs, axis=1)


def s5(u, prm, h0_re, h0_im):
    f32 = jnp.float32
    bsz, t_len, _ = u.shape
    uf = u.astype(f32)
    ug = uf.reshape(bsz, t_len, SSM_GROUPS, SSM_CH)
    a_re = prm['ssm_a_re'].astype(f32)
    a_im = prm['ssm_a_im'].astype(f32)
    dt = jnp.exp(prm['ssm_log_dt'].astype(f32))
    mag = jnp.exp(a_re * dt)
    ab_re = mag * jnp.cos(a_im * dt)
    ab_im = mag * jnp.sin(a_im * dt)
    den = a_re * a_re + a_im * a_im
    nr = ab_re - 1.0
    zr = (nr * a_re + ab_im * a_im) / den
    zi = (ab_im * a_re - nr * a_im) / den
    b_re = prm['ssm_b_re'].astype(f32)
    b_im = prm['ssm_b_im'].astype(f32)
    bb_re = zr[..., None] * b_re - zi[..., None] * b_im
    bb_im = zr[..., None] * b_im + zi[..., None] * b_re
    x_re = jnp.einsum('gpc,btgc->btgp', bb_re, ug)
    x_im = jnp.einsum('gpc,btgc->btgp', bb_im, ug)
    h0_re = h0_re.astype(f32)
    h0_im = h0_im.astype(f32)
    x_re = x_re.at[:, 0].add(ab_re * h0_re - ab_im * h0_im)
    x_im = x_im.at[:, 0].add(ab_re * h0_im + ab_im * h0_re)
    ar = jnp.broadcast_to(ab_re, x_re.shape)
    ai = jnp.broadcast_to(ab_im, x_re.shape)

    def combine(e1, e2):
        a1r, a1i, b1r, b1i = e1
        a2r, a2i, b2r, b2i = e2
        return (a2r * a1r - a2i * a1i, a2r * a1i + a2i * a1r,
                a2r * b1r - a2i * b1i + b2r, a2r * b1i + a2i * b1r + b2i)

    _, _, h_re, h_im = lax.associative_scan(combine, (ar, ai, x_re, x_im), axis=1)
    y = (jnp.einsum('gcp,btgp->btgc', prm['ssm_c_re'].astype(f32), h_re)
         - jnp.einsum('gcp,btgp->btgc', prm['ssm_c_im'].astype(f32), h_im))
    y = y.reshape(bsz, t_len, SSM_W) + prm['ssm_d'].astype(f32) * uf
    return y, h_re[:, -1], h_im[:, -1]


def cross_attn(h, mem_k, mem_v, w_q, w_o):
    bsz, t_len, _ = h.shape
    q = (h @ w_q).reshape(bsz, t_len, X_HEADS, X_DH)
    s = jnp.einsum('bthd,bmhd->bhtm', q.astype(jnp.float32), mem_k.astype(jnp.float32)) * (X_DH ** -0.5)
    p = jax.nn.softmax(s, axis=-1)
    o = jnp.einsum('bhtm,bmhd->bthd', p, mem_v.astype(jnp.float32)).reshape(bsz, t_len, D_MODEL)
    return o.astype(h.dtype) @ w_o


def conv_ffn(h, prm, prev):
    t_len = h.shape[1]
    a = h @ prm['w_ffn_a']
    ext = jnp.concatenate([prev.astype(a.dtype), a], axis=1)
    w = prm['ffn_conv_w']
    c = prm['ffn_conv_b'] + ext[:, 0:t_len] * w[0]
    for j in range(1, CONV_W):
        c = c + ext[:, j:j + t_len] * w[j]
    y = (jax.nn.silu(c) * (h @ prm['w_ffn_b'])) @ prm['w_ffn_down']
    return y, ext[:, ext.shape[1] - (CONV_W - 1):]


def layer(x, mem_k, mem_v, past_k, past_v, c0, n0, m0, s0_re, s0_im, conv_prev, prm):
    bsz, t_len, _ = x.shape
    h = rmsnorm(x, prm['ln_mix_g'])
    z = h @ prm['w_in'] + prm['b_in']
    q_a, k_a, v_a, o_a, i_a, f_a, q_b, k_b, v_b, u_c = split_cols(z)
    ml = (bsz, t_len, ML_HEADS, ML_DH)
    h_a, c_new, n_new, m_new = mlstm(q_a.reshape(ml), k_a.reshape(ml), v_a.reshape(ml), i_a, f_a, c0, n0, m0)
    y_a = head_rmsnorm(jax.nn.sigmoid(o_a) * h_a.astype(x.dtype), prm['gn_a_g'], ML_HEADS)
    sb = (bsz, t_len, SB_HEADS, SB_DH)
    q_b, k_b, v_b = q_b.reshape(sb), k_b.reshape(sb), v_b.reshape(sb)
    if past_k is None:
        o_b = stick_breaking_prompt(q_b, k_b, v_b)
    else:
        p_len = past_k.shape[1]
        k_all = jnp.concatenate([past_k.astype(k_b.dtype), k_b], axis=1)
        v_all = jnp.concatenate([past_v.astype(v_b.dtype), v_b], axis=1)
        o_b = stick_breaking_block(q_b, k_all, v_all, p_len + jnp.arange(t_len), jnp.arange(p_len + t_len))
    y_b = head_rmsnorm(o_b.reshape(bsz, t_len, SB_W), prm['gn_b_g'], SB_HEADS)
    y_s, s_re, s_im = s5(u_c, prm, s0_re, s0_im)
    g = jax.nn.gelu(y_s)
    y_c = rmsnorm(g * jax.nn.sigmoid(g @ prm['w_glu'].astype(jnp.float32) + prm['b_glu'].astype(jnp.float32)), prm['gn_c_g'])
    y_mix = jnp.concatenate([y_a, y_b.astype(x.dtype), y_c.astype(x.dtype)], axis=-1)
    x = x + y_mix @ prm['w_out']
    x = x + cross_attn(rmsnorm(x, prm['ln_x_g']), mem_k, mem_v, prm['w_xq'], prm['w_xo'])
    f, conv_new = conv_ffn(rmsnorm(x, prm['ln_ffn_g']), prm, conv_prev)
    x = x + f
    return x, k_b, v_b, c_new, n_new, m_new, s_re, s_im, conv_new


def setup_inputs(seed: int = 0) -> dict:
    key = jax.random.key(seed)
    ks = iter(jax.random.split(key, 64))
    f32 = jnp.float32

    def nrm(shape, scale=1.0):
        return scale * jax.random.normal(next(ks), shape, f32)

    def gain(shape):
        return 1.0 + 0.02 * nrm(shape)

    L = DEPTH
    f_off = 4 * ML_W + ML_HEADS
    b_in = 0.02 * nrm((L, N_IN))
    b_in = b_in.at[:, f_off:f_off + ML_HEADS].add(jnp.linspace(3.0, 6.0, ML_HEADS))
    a_im = jnp.pi * jnp.arange(SSM_P, dtype=f32)
    return {
        'x_prompt': nrm((BATCH, SEQ, D_MODEL)),
        'x_sample': nrm((DEC_BATCH, DEC_SEQ, D_MODEL)),
        'cache_sb_k': nrm((L, DEC_BATCH, PAST_LEN, SB_HEADS, SB_DH)),
        'cache_sb_v': nrm((L, DEC_BATCH, PAST_LEN, SB_HEADS, SB_DH)),
        'state_mlstm_c': nrm((L, DEC_BATCH, ML_HEADS, ML_DH, ML_DH), 0.3),
        'state_mlstm_n': nrm((L, DEC_BATCH, ML_HEADS, ML_DH), 0.3),
        'state_mlstm_m': nrm((L, DEC_BATCH, ML_HEADS)),
        'state_ssm_re': nrm((L, DEC_BATCH, SSM_GROUPS, SSM_P), 0.5),
        'state_ssm_im': nrm((L, DEC_BATCH, SSM_GROUPS, SSM_P), 0.5),
        'state_ffn_conv': nrm((L, DEC_BATCH, CONV_W - 1, D_FF)),
        'cache_mem_k': nrm((L, DEC_BATCH, N_MEM, X_HEADS, X_DH)),
        'cache_mem_v': nrm((L, DEC_BATCH, N_MEM, X_HEADS, X_DH)),
        'mem_prompt': nrm((BATCH, N_MEM, D_MODEL)),
        'ln_mix_g': gain((L, D_MODEL)),
        'w_in': nrm((L, D_MODEL, N_IN), D_MODEL ** -0.5),
        'b_in': b_in,
        'gn_a_g': gain((L, ML_W)),
        'gn_b_g': gain((L, SB_W)),
        'gn_c_g': gain((L, SSM_W)),
        'ssm_a_re': -0.5 + 0.01 * nrm((L, SSM_GROUPS, SSM_P)),
        'ssm_a_im': a_im + 0.01 * nrm((L, SSM_GROUPS, SSM_P)),
        'ssm_log_dt': jax.random.uniform(next(ks), (L, SSM_GROUPS, SSM_P), f32, minval=math.log(1e-3), maxval=math.log(1e-1)),
        'ssm_b_re': nrm((L, SSM_GROUPS, SSM_P, SSM_CH), (2 * SSM_CH) ** -0.5),
        'ssm_b_im': nrm((L, SSM_GROUPS, SSM_P, SSM_CH), (2 * SSM_CH) ** -0.5),
        'ssm_c_re': nrm((L, SSM_GROUPS, SSM_CH, SSM_P), (2 * SSM_P) ** -0.5),
        'ssm_c_im': nrm((L, SSM_GROUPS, SSM_CH, SSM_P), (2 * SSM_P) ** -0.5),
        'ssm_d': nrm((L, SSM_W)),
        'w_glu': nrm((L, SSM_W, SSM_W), SSM_W ** -0.5),
        'b_glu': 0.02 * nrm((L, SSM_W)),
        'w_out': nrm((L, MIX_W, D_MODEL), MIX_W ** -0.5),
        'ln_x_g': gain((L, D_MODEL)),
        'ln_mem_g': gain((L, D_MODEL)),
        'w_xq': nrm((L, D_MODEL, D_MODEL), D_MODEL ** -0.5),
        'w_xk': nrm((L, D_MODEL, D_MODEL), D_MODEL ** -0.5),
        'w_xv': nrm((L, D_MODEL, D_MODEL), D_MODEL ** -0.5),
        'w_xo': nrm((L, D_MODEL, D_MODEL), D_MODEL ** -0.5),
        'ln_ffn_g': gain((L, D_MODEL)),
        'w_ffn_a': nrm((L, D_MODEL, D_FF), D_MODEL ** -0.5),
        'w_ffn_b': nrm((L, D_MODEL, D_FF), D_MODEL ** -0.5),
        'ffn_conv_w': nrm((L, CONV_W, D_FF), CONV_W ** -0.5),
        'ffn_conv_b': 0.02 * nrm((L, D_FF)),
        'w_ffn_down': nrm((L, D_FF, D_MODEL), D_FF ** -0.5),
        'ln_f_g': gain((D_MODEL,)),
    }


def reference(x_prompt, x_sample, cache_sb_k, cache_sb_v, state_mlstm_c, state_mlstm_n, state_mlstm_m,
              state_ssm_re, state_ssm_im, state_ffn_conv, cache_mem_k, cache_mem_v, mem_prompt,
              ln_mix_g, w_in, b_in, gn_a_g, gn_b_g, gn_c_g, ssm_a_re, ssm_a_im, ssm_log_dt,
              ssm_b_re, ssm_b_im, ssm_c_re, ssm_c_im, ssm_d, w_glu, b_glu, w_out,
              ln_x_g, ln_mem_g, w_xq, w_xk, w_xv, w_xo,
              ln_ffn_g, w_ffn_a, w_ffn_b, ffn_conv_w, ffn_conv_b, w_ffn_down, ln_f_g):
    f32 = jnp.float32
    bp = x_prompt.shape[0]
    n_mem = mem_prompt.shape[1]
    xp, xs = x_prompt, x_sample
    p_states, s_states = [], []
    for l in range(DEPTH):
        prm = {'ln_mix_g': ln_mix_g[l], 'w_in': w_in[l], 'b_in': b_in[l],
               'gn_a_g': gn_a_g[l], 'gn_b_g': gn_b_g[l], 'gn_c_g': gn_c_g[l],
               'ssm_a_re': ssm_a_re[l], 'ssm_a_im': ssm_a_im[l], 'ssm_log_dt': ssm_log_dt[l],
               'ssm_b_re': ssm_b_re[l], 'ssm_b_im': ssm_b_im[l], 'ssm_c_re': ssm_c_re[l], 'ssm_c_im': ssm_c_im[l],
               'ssm_d': ssm_d[l], 'w_glu': w_glu[l], 'b_glu': b_glu[l], 'w_out': w_out[l],
               'ln_x_g': ln_x_g[l], 'w_xq': w_xq[l], 'w_xo': w_xo[l],
               'ln_ffn_g': ln_ffn_g[l], 'w_ffn_a': w_ffn_a[l], 'w_ffn_b': w_ffn_b[l],
               'ffn_conv_w': ffn_conv_w[l], 'ffn_conv_b': ffn_conv_b[l], 'w_ffn_down': w_ffn_down[l]}
        mn = rmsnorm(mem_prompt, ln_mem_g[l])
        mem_k = (mn @ w_xk[l]).reshape(bp, n_mem, X_HEADS, X_DH)
        mem_v = (mn @ w_xv[l]).reshape(bp, n_mem, X_HEADS, X_DH)
        xp, pk, pv, pc, pn, pm, pre, pim, pconv = layer(
            xp, mem_k, mem_v, None, None,
            jnp.zeros((bp, ML_HEADS, ML_DH, ML_DH), f32), jnp.zeros((bp, ML_HEADS, ML_DH), f32),
            jnp.zeros((bp, ML_HEADS), f32),
            jnp.zeros((bp, SSM_GROUPS, SSM_P), f32), jnp.zeros((bp, SSM_GROUPS, SSM_P), f32),
            jnp.zeros((bp, CONV_W - 1, D_FF), xp.dtype), prm)
        p_states.append((pk, pv, pc, pn, pm, pre, pim, pconv, mem_k, mem_v))
        xs, sk, sv, sc, sn, sm, sre, sim, sconv = layer(
            xs, cache_mem_k[l], cache_mem_v[l], cache_sb_k[l], cache_sb_v[l],
            state_mlstm_c[l], state_mlstm_n[l], state_mlstm_m[l],
            state_ssm_re[l], state_ssm_im[l], state_ffn_conv[l], prm)
        s_states.append((sk, sv, sc, sn, sm, sre, sim, sconv))
    (p_sb_k, p_sb_v, p_mlstm_c, p_mlstm_n, p_mlstm_m, p_ssm_re, p_ssm_im, p_ffn_conv,
     p_mem_k, p_mem_v) = [jnp.stack(a) for a in zip(*p_states)]
    (s_sb_k, s_sb_v, s_mlstm_c, s_mlstm_n, s_mlstm_m, s_ssm_re, s_ssm_im,
     s_ffn_conv) = [jnp.stack(a) for a in zip(*s_states)]
    y_prompt = rmsnorm(xp, ln_f_g)
    y_sample = rmsnorm(xs, ln_f_g)
    return (y_prompt, y_sample,
            p_sb_k, p_sb_v, p_mlstm_c, p_mlstm_n, p_mlstm_m, p_ssm_re, p_ssm_im, p_ffn_conv, p_mem_k, p_mem_v,
            s_sb_k, s_sb_v, s_mlstm_c, s_mlstm_n, s_mlstm_m, s_ssm_re, s_ssm_im, s_ffn_conv)
```

```python
import functools
import math

import jax
import jax.numpy as jnp
from jax import lax
from jax.experimental import pallas as pl
from jax.experimental.pallas import tpu as pltpu

F32 = jnp.float32
BF16 = jnp.bfloat16

D_MODEL = 1024
EPS = 1e-6
CHUNK = 64
ML_HEADS = 4
ML_DH = 96
ML_DHP = 128
ML_W = ML_HEADS * ML_DH
ML_WP = ML_HEADS * ML_DHP
SB_DH = 64
SB_HEADS = 6
SB_W = SB_HEADS * SB_DH
SB_PAIRS = SB_HEADS // 2
SSM_W = 256
SSM_CH = 16
SSM_GROUPS = 16
SSM_P = 64
SSM_N = SSM_GROUPS * SSM_P
X_HEADS = 4
X_DH = 256
N_MEM = 256
D_FF = 2816
CONV_W = 3
FF_CHUNK = 256
KEY_BLOCK = 128
LANE = 128
VMEM_LIMIT = 56 * 1024 * 1024

_OFF_QB = 4 * ML_WP
_OFF_KB = _OFF_QB + SB_W
_OFF_VB = _OFF_KB + SB_W
_OFF_UC = _OFF_VB + SB_W
_N_MAIN = _OFF_UC + SSM_W


def _cparams(*sem):
    return pltpu.CompilerParams(dimension_semantics=sem, vmem_limit_bytes=VMEM_LIMIT)


def _const_spec(shape):
    nd = len(shape)
    return pl.BlockSpec(shape, lambda *_: (0,) * nd, pipeline_mode=pl.Buffered(1))


def _rmsnorm(x, g):
    ms = jnp.mean(x * x, axis=-1, keepdims=True)
    return x * lax.rsqrt(ms + EPS) * g


def _sigmoid(x):
    return 1.0 / (1.0 + jnp.exp(-x))


def _log_sigmoid(x):
    return jnp.minimum(x, 0.0) - jnp.log(1.0 + jnp.exp(-jnp.abs(x)))


def _dot(a, b):
    return jnp.dot(a, b, preferred_element_type=F32)


def _dot_nt(a, b):
    return lax.dot_general(a, b, (((1,), (1,)), ((), ())), preferred_element_type=F32)


def _dot_tn(a, b):
    return lax.dot_general(a, b, (((0,), (0,)), ((), ())), preferred_element_type=F32)


def _split_dot(x, m_bf16, terms):
    acc = None
    r = x
    for i in range(terms):
        p = r.astype(BF16)
        d = _dot(p, m_bf16)
        acc = d if acc is None else acc + d
        if i + 1 < terms:
            r = r - p.astype(F32)
    return acc


def _in_proj_body(x_ref, g_ref, w_ref, b_ref, wg_ref, bg_ref,
                  za_ref, qb_ref, kb_ref, vb_ref, uc_ref, gt_ref):
    h = _rmsnorm(x_ref[...], g_ref[...]).astype(BF16)

    def proj(lo, hi):
        return _dot(h, w_ref[:, lo:hi]) + b_ref[:, lo:hi]

    za_ref[...] = proj(0, _OFF_QB)
    qb_ref[...] = proj(_OFF_QB, _OFF_KB)
    kb_ref[...] = proj(_OFF_KB, _OFF_VB)
    vb_ref[...] = proj(_OFF_VB, _OFF_UC)
    uc_ref[...] = proj(_OFF_UC, _N_MAIN)
    gt_ref[0] = _dot_nt(wg_ref[...], h) + bg_ref[...]


def _in_proj(x, g, w, b, wg, bg, bsz, t_len, tm):
    n = bsz * t_len
    nt = t_len // tm
    row = lambda width: pl.BlockSpec((tm, width), lambda bi, ti: (bi * nt + ti, 0))
    return pl.pallas_call(
        _in_proj_body,
        grid=(bsz, nt),
        in_specs=[row(D_MODEL), _const_spec((1, D_MODEL)), _const_spec((D_MODEL, _N_MAIN)),
                  _const_spec((1, _N_MAIN)), _const_spec((8, D_MODEL)), _const_spec((8, 1))],
        out_specs=[row(4 * ML_WP), row(SB_W), row(SB_W), row(SB_W),
                   pl.BlockSpec((tm, SSM_W), lambda bi, ti: (ti, bi)),
                   pl.BlockSpec((1, 8, tm), lambda bi, ti: (bi * nt + ti, 0, 0))],
        out_shape=[jax.ShapeDtypeStruct((n, 4 * ML_WP), F32),
                   jax.ShapeDtypeStruct((n, SB_W), F32),
                   jax.ShapeDtypeStruct((n, SB_W), F32),
                   jax.ShapeDtypeStruct((n, SB_W), F32),
                   jax.ShapeDtypeStruct((t_len, bsz * SSM_W), F32),
                   jax.ShapeDtypeStruct((bsz * nt, 8, tm), F32)],
        compiler_params=_cparams("parallel", "parallel"),
        name="in_proj",
    )(x, g, w, b, wg, bg)


def _mlstm_body(blk, q_ref, k_ref, v_ref, o_ref, g_ref, c0_ref, n0_ref, m0_ref, gn_ref,
                ya_ref, c_ref, n_ref, m_ref):
    @pl.when(pl.program_id(1) == 0)
    def _():
        c_ref[...] = c0_ref[...]
        n_ref[...] = n0_ref[...]
        m_ref[...] = m0_ref[...]

    ts = q_ref.shape[0]
    t_i = lax.broadcasted_iota(jnp.int32, (blk, blk), 0)
    s_i = lax.broadcasted_iota(jnp.int32, (blk, blk), 1)
    causal = s_i <= t_i
    eye = s_i == t_i
    upper = jnp.where(t_i <= s_i, 1.0, 0.0).astype(BF16)
    k_scale = ML_DH ** -0.5

    for ci in range(ts // blk):
        r0 = ci * blk
        g = g_ref[0, :, r0:r0 + blk]
        lf_all = _log_sigmoid(g)
        b_rows = _split_dot(lf_all, upper, 3)
        for h in range(ML_HEADS):
            c0, c1 = h * ML_DHP, (h + 1) * ML_DHP
            q = q_ref[r0:r0 + blk, c0:c1]
            k = k_ref[r0:r0 + blk, c0:c1] * k_scale
            v = v_ref[r0:r0 + blk, c0:c1]
            qb, kb, vb = q.astype(BF16), k.astype(BF16), v.astype(BF16)
            li = g[h:h + 1, :]
            lf = lf_all[4 + h:5 + h, :]
            a_row = li - b_rows[4 + h:5 + h, :]
            b_col = jnp.sum(jnp.where(causal, lf, 0.0), axis=-1, keepdims=True)
            m_prev = m_ref[0, h][:, 0:1]
            n_row = n_ref[0, h]
            ct = c_ref[0, h]

            a_m = jnp.where(causal, a_row, -jnp.inf)
            m_col = jnp.maximum(jnp.max(a_m, axis=-1, keepdims=True), m_prev)
            w_intra = jnp.exp(a_m - m_col)
            w_inter = jnp.exp(m_prev - m_col)
            s = _dot_nt(qb, kb) * w_intra
            num = _dot(s.astype(BF16), vb) + w_inter * _dot(qb, ct.astype(BF16))
            den = (jnp.sum(s, axis=-1, keepdims=True)
                   + w_inter * jnp.sum(q * n_row, axis=-1, keepdims=True))
            hh = num / jnp.maximum(jnp.abs(den), jnp.exp(-(b_col + m_col)))

            x = _sigmoid(o_ref[r0:r0 + blk, c0:c1]) * hh
            ms = jnp.sum(x * x, axis=-1, keepdims=True) * (1.0 / ML_DH)
            ya_ref[r0:r0 + blk, c0:c1] = x * lax.rsqrt(ms + EPS) * gn_ref[:, c0:c1]

            m_last = m_col[blk - 1:blk, :]
            wg_row = jnp.exp(a_row - m_last)
            wg_col = jnp.sum(jnp.where(eye, wg_row, 0.0), axis=-1, keepdims=True)
            decay = jnp.exp(m_prev - m_last)
            c_ref[0, h] = decay * ct + _dot_tn((k * wg_col).astype(BF16), vb)
            wg8 = jnp.broadcast_to(wg_row, (8, blk)).astype(BF16)
            n_ref[0, h] = decay * n_row + _dot(wg8, kb)[0:1, :]
            m_ref[0, h] = jnp.broadcast_to(b_col[blk - 1:blk, :] + m_last, (1, LANE))


def _mlstm(za, o_src, gates, c0, n0, m0, gn, bsz, t_len, ts, blk):
    n = bsz * t_len
    ns = t_len // ts
    col = lambda j: pl.BlockSpec((ts, ML_WP), lambda bi, si: (bi * ns + si, j))
    st4 = pl.BlockSpec((1, ML_HEADS, ML_DHP, ML_DHP), lambda bi, si: (bi, 0, 0, 0))
    st3 = pl.BlockSpec((1, ML_HEADS, 1, LANE), lambda bi, si: (bi, 0, 0, 0))
    return pl.pallas_call(
        functools.partial(_mlstm_body, blk),
        grid=(bsz, ns),
        in_specs=[col(0), col(1), col(2), col(3),
                  pl.BlockSpec((1, 8, ts), lambda bi, si: (bi * ns + si, 0, 0)),
                  st4, st3, st3, _const_spec((1, ML_WP))],
        out_specs=[pl.BlockSpec((ts, ML_WP), lambda bi, si: (bi * ns + si, 0)), st4, st3, st3],
        out_shape=[jax.ShapeDtypeStruct((n, ML_WP), F32),
                   jax.ShapeDtypeStruct((bsz, ML_HEADS, ML_DHP, ML_DHP), F32),
                   jax.ShapeDtypeStruct((bsz, ML_HEADS, 1, LANE), F32),
                   jax.ShapeDtypeStruct((bsz, ML_HEADS, 1, LANE), F32)],
        compiler_params=_cparams("parallel", "arbitrary"),
        name="mlstm",
    )(za, za, za, o_src, gates, c0, n0, m0, gn)


def _sb_body(past_off, q_ref, kc_ref, vc_ref, kp_ref, vp_ref, gn_ref, y_ref, acc_ref, car_ref):
    tq = q_ref.shape[1]
    qi = pl.program_id(2)
    lane = lax.broadcasted_iota(jnp.int32, (1, LANE), 1)
    first = lane < SB_DH
    q = q_ref[0] * (SB_DH ** -0.5)
    qh = [jnp.where(first, q, 0.0).astype(BF16), jnp.where(first, 0.0, q).astype(BF16)]

    def ones_and_after(kb):
        r = lax.broadcasted_iota(jnp.int32, (kb, LANE + kb), 0)
        c = lax.broadcasted_iota(jnp.int32, (kb, LANE + kb), 1)
        return jnp.where((c < LANE) | (r > c - LANE), 1.0, 0.0).astype(BF16)

    def block(k, v, first_block):
        kb = k.shape[0]
        kbf, vbf = k.astype(BF16), v.astype(BF16)
        uo = ones_and_after(kb)
        if first_block:
            t_i = lax.broadcasted_iota(jnp.int32, (tq, kb), 0)
            s_i = lax.broadcasted_iota(jnp.int32, (tq, kb), 1)
            vis = s_i < t_i
        for hd in range(2):
            z = _dot_nt(qh[hd], kbf)
            l1m = -(jnp.maximum(z, 0.0) + jnp.log(1.0 + jnp.exp(-jnp.abs(z))))
            lmask = jnp.where(vis, l1m, 0.0) if first_block else l1m
            cs = _split_dot(lmask, uo, 2)
            if first_block:
                after = cs[:, LANE:]
                a = jnp.where(vis, jnp.exp(z + l1m + after), 0.0)
                acc_ref[hd] = _dot(a.astype(BF16), vbf)
                car_ref[hd] = cs[:, :LANE]
            else:
                after = car_ref[hd][:, :kb] + cs[:, LANE:]
                a = jnp.exp(z + l1m + after)
                acc_ref[hd] += _dot(a.astype(BF16), vbf)
                car_ref[hd] += cs[:, :LANE]

    block(kc_ref[0], vc_ref[0], True)

    n_past = (past_off + qi * tq) // KEY_BLOCK

    def past(j, carry):
        start = pl.multiple_of((n_past - 1 - j) * KEY_BLOCK, KEY_BLOCK)
        block(kp_ref[0, pl.ds(start, KEY_BLOCK), :], vp_ref[0, pl.ds(start, KEY_BLOCK), :], False)
        return carry

    lax.fori_loop(0, n_past, past, 0)

    o = jnp.where(first, acc_ref[0], acc_ref[1])
    o2 = o * o
    ss0 = jnp.sum(jnp.where(first, o2, 0.0), axis=-1, keepdims=True)
    ss1 = jnp.sum(jnp.where(first, 0.0, o2), axis=-1, keepdims=True)
    inv = jnp.where(first, lax.rsqrt(ss0 * (1.0 / SB_DH) + EPS), lax.rsqrt(ss1 * (1.0 / SB_DH) + EPS))
    y_ref[0] = o * inv * gn_ref[0]


def _sb_attn(q, kc, vc, kp, vp, gn, tq, past_off):
    bsz, t_len, _ = q.shape
    p_len = kp.shape[1]
    cur = pl.BlockSpec((1, tq, LANE), lambda bi, pi, qi: (bi, qi, pi))
    past = pl.BlockSpec((1, p_len, LANE), lambda bi, pi, qi: (bi, 0, pi))
    return pl.pallas_call(
        functools.partial(_sb_body, past_off),
        grid=(bsz, SB_PAIRS, t_len // tq),
        in_specs=[cur, cur, cur, past, past, pl.BlockSpec((1, 1, LANE), lambda bi, pi, qi: (pi, 0, 0))],
        out_specs=cur,
        out_shape=jax.ShapeDtypeStruct((bsz, t_len, SB_W), F32),
        scratch_shapes=[pltpu.VMEM((2, tq, LANE), F32), pltpu.VMEM((2, tq, LANE), F32)],
        compiler_params=_cparams("parallel", "parallel", "arbitrary"),
        name="sb_attn",
    )(q, kc, vc, kp, vp, gn)


def _s5_prep_body(are_ref, aim_ref, ldt_ref, bre_ref, bim_ref, ab_ref, bcat_ref):
    a_re, a_im = are_ref[...], aim_ref[...]
    dt = jnp.exp(ldt_ref[...])
    mag = jnp.exp(a_re * dt)
    ab_re = mag * jnp.cos(a_im * dt)
    ab_im = mag * jnp.sin(a_im * dt)
    den = a_re * a_re + a_im * a_im
    nr = ab_re - 1.0
    zr = (nr * a_re + ab_im * a_im) / den
    zi = (ab_im * a_re - nr * a_im) / den
    ab_ref[0:1, :] = ab_re
    ab_ref[1:2, :] = ab_im
    b_re, b_im = bre_ref[...], bim_ref[...]
    bcat_ref[:, :SSM_N] = (zr * b_re - zi * b_im).astype(BF16)
    bcat_ref[:, SSM_N:] = (zr * b_im + zi * b_re).astype(BF16)


def _s5_prep(a_re, a_im, log_dt, b_re_blk, b_im_blk):
    return pl.pallas_call(
        _s5_prep_body,
        out_shape=[jax.ShapeDtypeStruct((2, SSM_N), F32), jax.ShapeDtypeStruct((SSM_W, 2 * SSM_N), BF16)],
        name="s5_prep",
    )(a_re, a_im, log_dt, b_re_blk, b_im_blk)


def _gelu_tanh(x):
    return 0.5 * x * (1.0 + jnp.tanh(math.sqrt(2.0 / math.pi) * (x + 0.044715 * (x * x * x))))


def _s5_body(u_ref, bcat_ref, ab_ref, cre_ref, cim_ref, d_ref, wglu_ref, bglu_ref, gn_ref,
             h0r_ref, h0i_ref, y_ref, hr_ref, hi_ref, xs_ref):
    tt, bsz, _ = u_ref.shape

    @pl.when(pl.program_id(0) == 0)
    def _():
        hr_ref[...] = h0r_ref[...]
        hi_ref[...] = h0i_ref[...]

    u2 = u_ref[...].reshape(tt * bsz, SSM_W)
    xs_ref[...] = _dot(u2.astype(BF16), bcat_ref[...])
    ar = jnp.broadcast_to(ab_ref[0:1, :], (bsz, SSM_N))
    ai = jnp.broadcast_to(ab_ref[1:2, :], (bsz, SSM_N))

    def step(t, carry):
        hr, hi = carry
        off = pl.multiple_of(t * bsz, bsz)
        xr = xs_ref[pl.ds(off, bsz), :SSM_N]
        xi = xs_ref[pl.ds(off, bsz), SSM_N:]
        nr = ar * hr - ai * hi + xr
        ni = ar * hi + ai * hr + xi
        xs_ref[pl.ds(off, bsz), :SSM_N] = nr
        xs_ref[pl.ds(off, bsz), SSM_N:] = ni
        return nr, ni

    hr, hi = lax.fori_loop(0, tt, step, (hr_ref[...], hi_ref[...]))
    hr_ref[...] = hr
    hi_ref[...] = hi

    y = (_dot(xs_ref[:, :SSM_N].astype(BF16), cre_ref[...])
         - _dot(xs_ref[:, SSM_N:].astype(BF16), cim_ref[...])
         + d_ref[...] * u2)
    g = _gelu_tanh(y)
    gate = _sigmoid(_dot(g.astype(BF16), wglu_ref[...]) + bglu_ref[...])
    y_ref[...] = _rmsnorm(g * gate, gn_ref[...]).reshape(tt, bsz, SSM_W)


def _s5(u_tm, bcat, ab, c_re, c_im, d, w_glu, b_glu, gn, h0r, h0i, tt):
    t_len, bsz, _ = u_tm.shape
    tile = pl.BlockSpec((tt, bsz, SSM_W), lambda ti: (ti, 0, 0))
    st = pl.BlockSpec((bsz, SSM_N), lambda ti: (0, 0))
    return pl.pallas_call(
        _s5_body,
        grid=(t_len // tt,),
        in_specs=[tile, _const_spec((SSM_W, 2 * SSM_N)), _const_spec((2, SSM_N)),
                  _const_spec((SSM_N, SSM_W)), _const_spec((SSM_N, SSM_W)), _const_spec((1, SSM_W)),
                  _const_spec((SSM_W, SSM_W)), _const_spec((1, SSM_W)), _const_spec((1, SSM_W)), st, st],
        out_specs=[tile, st, st],
        out_shape=[jax.ShapeDtypeStruct((t_len, bsz, SSM_W), F32),
                   jax.ShapeDtypeStruct((bsz, SSM_N), F32),
                   jax.ShapeDtypeStruct((bsz, SSM_N), F32)],
        scratch_shapes=[pltpu.VMEM((tt * bsz, 2 * SSM_N), F32)],
        compiler_params=_cparams("arbitrary"),
        name="s5",
    )(u_tm, bcat, ab, c_re, c_im, d, w_glu, b_glu, gn, h0r, h0i)


def _out_proj_body(x_ref, ya_ref, yb_ref, yc_ref, woa_ref, wob_ref, woc_ref, lnx_ref, wxq_ref,
                   x1_ref, qx_ref):
    x1 = (x_ref[...]
          + _dot(ya_ref[...].astype(BF16), woa_ref[...])
          + _dot(yb_ref[...].astype(BF16), wob_ref[...])
          + _dot(yc_ref[...].astype(BF16), woc_ref[...]))
    x1_ref[...] = x1
    hq = _rmsnorm(x1, lnx_ref[...]).astype(BF16)
    qx_ref[...] = _dot(hq, wxq_ref[...]).astype(BF16)


def _out_proj(x, ya, yb, yc_tm, woa, wob, woc, lnx, wxq, bsz, t_len, tm):
    n = bsz * t_len
    nt = t_len // tm
    row = lambda width: pl.BlockSpec((tm, width), lambda bi, ti: (bi * nt + ti, 0))
    return pl.pallas_call(
        _out_proj_body,
        grid=(bsz, nt),
        in_specs=[row(D_MODEL), row(ML_WP), row(SB_W),
                  pl.BlockSpec((tm, SSM_W), lambda bi, ti: (ti, bi)),
                  _const_spec((ML_WP, D_MODEL)), _const_spec((SB_W, D_MODEL)), _const_spec((SSM_W, D_MODEL)),
                  _const_spec((1, D_MODEL)), _const_spec((D_MODEL, D_MODEL))],
        out_specs=[row(D_MODEL), row(D_MODEL)],
        out_shape=[jax.ShapeDtypeStruct((n, D_MODEL), F32), jax.ShapeDtypeStruct((n, D_MODEL), BF16)],
        compiler_params=_cparams("parallel", "parallel"),
        name="out_proj",
    )(x, ya, yb, yc_tm, woa, wob, woc, lnx, wxq)


def _mem_proj_body(m_ref, g_ref, wk_ref, wv_ref, k_ref, v_ref):
    mn = _rmsnorm(m_ref[...], g_ref[...]).astype(BF16)
    k_ref[...] = _dot(mn, wk_ref[...])
    v_ref[...] = _dot(mn, wv_ref[...])


def _mem_proj(mem, g, wk, wv, tm):
    n = mem.shape[0]
    row = pl.BlockSpec((tm, D_MODEL), lambda i: (i, 0))
    return pl.pallas_call(
        _mem_proj_body,
        grid=(n // tm,),
        in_specs=[row, _const_spec((1, D_MODEL)), _const_spec((D_MODEL, D_MODEL)), _const_spec((D_MODEL, D_MODEL))],
        out_specs=[row, row],
        out_shape=[jax.ShapeDtypeStruct((n, D_MODEL), F32)] * 2,
        compiler_params=_cparams("parallel"),
        name="mem_proj",
    )(mem, g, wk, wv)


def _x_attn_body(x_ref, q_ref, k_ref, v_ref, wo_ref, o_ref):
    acc = x_ref[...]
    for h in range(X_HEADS):
        c0, c1 = h * X_DH, (h + 1) * X_DH
        kh = k_ref[0, :, c0:c1].astype(BF16)
        vh = v_ref[0, :, c0:c1].astype(BF16)
        s = _dot_nt(q_ref[:, c0:c1], kh) * (X_DH ** -0.5)
        e = jnp.exp(s - jnp.max(s, axis=-1, keepdims=True))
        p = e / jnp.sum(e, axis=-1, keepdims=True)
        oh = _dot(p.astype(BF16), vh)
        acc = acc + _dot(oh.astype(BF16), wo_ref[c0:c1, :])
    o_ref[...] = acc


def _x_attn(x1, qx, mem_k, mem_v, wo, bsz, t_len, tq):
    n = bsz * t_len
    nt = t_len // tq
    row = pl.BlockSpec((tq, D_MODEL), lambda bi, ti: (bi * nt + ti, 0))
    mem = pl.BlockSpec((1, N_MEM, D_MODEL), lambda bi, ti: (bi, 0, 0))
    return pl.pallas_call(
        _x_attn_body,
        grid=(bsz, nt),
        in_specs=[row, row, mem, mem, _const_spec((D_MODEL, D_MODEL))],
        out_specs=row,
        out_shape=jax.ShapeDtypeStruct((n, D_MODEL), F32),
        compiler_params=_cparams("parallel", "parallel"),
        name="x_attn",
    )(x1, qx, mem_k, mem_v, wo)


def _conv_ffn_body(final_norm, x_ref, g_ref, wa_ref, wb_ref, cw_ref, cb_ref, wd_ref, prev_ref, gf_ref,
                   o_ref, conv_ref, abuf_ref, acc_ref):
    tm = x_ref.shape[0]

    @pl.when(pl.program_id(1) == 0)
    def _():
        conv_ref[...] = prev_ref[...]

    x = x_ref[...]
    hb = _rmsnorm(x, g_ref[...]).astype(BF16)
    acc_ref[...] = x
    for c in range(D_FF // FF_CHUNK):
        c0, c1 = c * FF_CHUNK, (c + 1) * FF_CHUNK
        a = _dot(hb, wa_ref[:, c0:c1])
        abuf_ref[6:8, :] = conv_ref[0, :, c0:c1]
        abuf_ref[8:8 + tm, :] = a
        conv_ref[0, :, c0:c1] = a[tm - 2:tm, :]
        cv = (cb_ref[:, c0:c1]
              + abuf_ref[6:6 + tm, :] * cw_ref[0:1, c0:c1]
              + abuf_ref[7:7 + tm, :] * cw_ref[1:2, c0:c1]
              + a * cw_ref[2:3, c0:c1])
        y = cv * _sigmoid(cv) * _dot(hb, wb_ref[:, c0:c1])
        acc_ref[...] += _dot(y.astype(BF16), wd_ref[c0:c1, :])
    out = acc_ref[...]
    if final_norm:
        out = _rmsnorm(out, gf_ref[...])
    o_ref[...] = out


def _conv_ffn(x, g, wa, wb, cw, cb, wd, prev, gf, bsz, t_len, tm, final_norm):
    n = bsz * t_len
    nt = t_len // tm
    row = pl.BlockSpec((tm, D_MODEL), lambda bi, ti: (bi * nt + ti, 0))
    conv = pl.BlockSpec((1, CONV_W - 1, D_FF), lambda bi, ti: (bi, 0, 0))
    return pl.pallas_call(
        functools.partial(_conv_ffn_body, final_norm),
        grid=(bsz, nt),
        in_specs=[row, _const_spec((1, D_MODEL)), _const_spec((D_MODEL, D_FF)), _const_spec((D_MODEL, D_FF)),
                  _const_spec((CONV_W, D_FF)), _const_spec((1, D_FF)), _const_spec((D_FF, D_MODEL)),
                  conv, _const_spec((1, D_MODEL))],
        out_specs=[row, conv],
        out_shape=[jax.ShapeDtypeStruct((n, D_MODEL), F32),
                   jax.ShapeDtypeStruct((bsz, CONV_W - 1, D_FF), F32)],
        scratch_shapes=[pltpu.VMEM((tm + 8, FF_CHUNK), F32), pltpu.VMEM((tm, D_MODEL), F32)],
        compiler_params=_cparams("parallel", "arbitrary"),
        name="conv_ffn",
    )(x, g, wa, wb, cw, cb, wd, prev, gf)


def _pad_heads(a, axis):
    shp = a.shape
    a = a.reshape(shp[:axis] + (ML_HEADS, ML_DH) + shp[axis + 1:])
    pad = [(0, 0)] * a.ndim
    pad[axis + 1] = (0, ML_DHP - ML_DH)
    a = jnp.pad(a, pad)
    return a.reshape(shp[:axis] + (ML_WP,) + shp[axis + 1:])


def _block_diag(blocks):
    g, r, c = blocks.shape
    eye = jnp.eye(g, dtype=blocks.dtype)
    return (eye[:, None, :, None] * blocks[:, :, None, :]).reshape(g * r, g * c)


def _pack_layer(p):
    w_in, b_in = p["w_in"], p["b_in"]
    o = 0
    parts_w, parts_b = [], []
    for _ in range(4):
        parts_w.append(_pad_heads(w_in[:, o:o + ML_W], 1))
        parts_b.append(_pad_heads(b_in[o:o + ML_W], 0))
        o += ML_W
    wg = w_in[:, o:o + 2 * ML_HEADS].T
    bg = b_in[o:o + 2 * ML_HEADS][:, None]
    o += 2 * ML_HEADS
    parts_w.append(w_in[:, o:])
    parts_b.append(b_in[o:])
    w_out = p["w_out"]
    return dict(
        ln_mix_g=p["ln_mix_g"][None], w_main=jnp.concatenate(parts_w, axis=1).astype(BF16),
        b_main=jnp.concatenate(parts_b)[None], wg=wg.astype(BF16), bg=bg,
        gn_a=_pad_heads(p["gn_a_g"], 0)[None], gn_b=p["gn_b_g"].reshape(SB_PAIRS, 1, LANE), gn_c=p["gn_c_g"][None],
        a_re=p["ssm_a_re"].reshape(1, SSM_N), a_im=p["ssm_a_im"].reshape(1, SSM_N),
        log_dt=p["ssm_log_dt"].reshape(1, SSM_N),
        b_re_blk=_block_diag(jnp.swapaxes(p["ssm_b_re"], 1, 2)),
        b_im_blk=_block_diag(jnp.swapaxes(p["ssm_b_im"], 1, 2)),
        c_re_blk=_block_diag(jnp.swapaxes(p["ssm_c_re"], 1, 2)).astype(BF16),
        c_im_blk=_block_diag(jnp.swapaxes(p["ssm_c_im"], 1, 2)).astype(BF16),
        ssm_d=p["ssm_d"][None], w_glu=p["w_glu"].astype(BF16), b_glu=p["b_glu"][None],
        w_out_a=_pad_heads(w_out[:ML_W], 0).astype(BF16), w_out_b=w_out[ML_W:ML_W + SB_W].astype(BF16),
        w_out_c=w_out[ML_W + SB_W:].astype(BF16),
        ln_x_g=p["ln_x_g"][None], w_xq=p["w_xq"].astype(BF16), w_xo=p["w_xo"].astype(BF16),
        ln_mem_g=p["ln_mem_g"][None], w_xk=p["w_xk"].astype(BF16), w_xv=p["w_xv"].astype(BF16),
        ln_ffn_g=p["ln_ffn_g"][None], w_ffn_a=p["w_ffn_a"].astype(BF16), w_ffn_b=p["w_ffn_b"].astype(BF16),
        ffn_conv_w=p["ffn_conv_w"], ffn_conv_b=p["ffn_conv_b"][None], w_ffn_down=p["w_ffn_down"].astype(BF16),
    )


def _layer(x, w, ssm, mem_k, mem_v, past_k, past_v, c0, n0, m0, s0_re, s0_im, conv_prev, ln_f, cfg, final_norm):
    bsz, t_len = cfg["bsz"], cfg["t_len"]
    tm, ts, blk = cfg["tm"], cfg["ts"], cfg["blk"]
    ab, bcat = ssm

    za, qb, kb, vb, u_tm, gates = _in_proj(x, w["ln_mix_g"], w["w_main"], w["b_main"], w["wg"], w["bg"], bsz, t_len, tm)
    if tm != ts:
        gates = gates.reshape(-1, 8, tm // ts, ts).transpose(0, 2, 1, 3).reshape(-1, 8, ts)

    ct0 = jnp.pad(jnp.swapaxes(c0, -1, -2), ((0, 0), (0, 0), (0, ML_DHP - ML_DH), (0, ML_DHP - ML_DH)))
    n0p = jnp.pad(n0, ((0, 0), (0, 0), (0, ML_DHP - ML_DH)))[:, :, None, :]
    m0p = jnp.broadcast_to(m0[:, :, None, None], (bsz, ML_HEADS, 1, LANE))
    ya, ct, n_new, m_new = _mlstm(za, za, gates, ct0, n0p, m0p, w["gn_a"], bsz, t_len, ts, blk)

    q3, k3, v3 = (a.reshape(bsz, t_len, SB_W) for a in (qb, kb, vb))
    if past_k is None:
        yb = _sb_attn(q3, k3, v3, k3, v3, w["gn_b"], cfg["tq_sb"], 0)
    else:
        p_len = past_k.shape[1]
        yb = _sb_attn(q3, k3, v3, past_k.reshape(bsz, p_len, SB_W), past_v.reshape(bsz, p_len, SB_W),
                      w["gn_b"], cfg["tq_sb"], p_len)
    yb = yb.reshape(bsz * t_len, SB_W)

    yc_tm, s_re, s_im = _s5(u_tm.reshape(t_len, bsz, SSM_W), bcat, ab, w["c_re_blk"], w["c_im_blk"], w["ssm_d"],
                            w["w_glu"], w["b_glu"], w["gn_c"], s0_re.reshape(bsz, SSM_N), s0_im.reshape(bsz, SSM_N),
                            cfg["tt"])

    x1, qx = _out_proj(x, ya, yb, yc_tm.reshape(t_len, bsz * SSM_W), w["w_out_a"], w["w_out_b"], w["w_out_c"],
                       w["ln_x_g"], w["w_xq"], bsz, t_len, tm)
    x2 = _x_attn(x1, qx, mem_k.reshape(bsz, N_MEM, D_MODEL), mem_v.reshape(bsz, N_MEM, D_MODEL), w["w_xo"],
                 bsz, t_len, tm)
    x3, conv_new = _conv_ffn(x2, w["ln_ffn_g"], w["w_ffn_a"], w["w_ffn_b"], w["ffn_conv_w"], w["ffn_conv_b"],
                             w["w_ffn_down"], conv_prev, ln_f, bsz, t_len, cfg["tm_ffn"], final_norm)

    states = (kb.reshape(bsz, t_len, SB_HEADS, SB_DH), vb.reshape(bsz, t_len, SB_HEADS, SB_DH),
              jnp.swapaxes(ct[:, :, :ML_DH, :ML_DH], -1, -2), n_new[:, :, 0, :ML_DH], m_new[:, :, 0, 0],
              s_re.reshape(bsz, SSM_GROUPS, SSM_P), s_im.reshape(bsz, SSM_GROUPS, SSM_P), conv_new)
    return x3, states


_PROMPT_CFG = dict(tm=512, ts=128, blk=CHUNK, tq_sb=128, tt=64, tm_ffn=256)
_SAMPLE_CFG = dict(tm=32, ts=32, blk=32, tq_sb=32, tt=32, tm_ffn=32)


def kernel(x_prompt, x_sample, cache_sb_k, cache_sb_v, state_mlstm_c, state_mlstm_n, state_mlstm_m, state_ssm_re, state_ssm_im, state_ffn_conv, cache_mem_k, cache_mem_v, mem_prompt, ln_mix_g, w_in, b_in, gn_a_g, gn_b_g, gn_c_g, ssm_a_re, ssm_a_im, ssm_log_dt, ssm_b_re, ssm_b_im, ssm_c_re, ssm_c_im, ssm_d, w_glu, b_glu, w_out, ln_x_g, ln_mem_g, w_xq, w_xk, w_xv, w_xo, ln_ffn_g, w_ffn_a, w_ffn_b, ffn_conv_w, ffn_conv_b, w_ffn_down, ln_f_g):
    per_layer = dict(ln_mix_g=ln_mix_g, w_in=w_in, b_in=b_in, gn_a_g=gn_a_g, gn_b_g=gn_b_g, gn_c_g=gn_c_g,
                     ssm_a_re=ssm_a_re, ssm_a_im=ssm_a_im, ssm_log_dt=ssm_log_dt, ssm_b_re=ssm_b_re,
                     ssm_b_im=ssm_b_im, ssm_c_re=ssm_c_re, ssm_c_im=ssm_c_im, ssm_d=ssm_d, w_glu=w_glu,
                     b_glu=b_glu, w_out=w_out, ln_x_g=ln_x_g, ln_mem_g=ln_mem_g, w_xq=w_xq, w_xk=w_xk,
                     w_xv=w_xv, w_xo=w_xo, ln_ffn_g=ln_ffn_g, w_ffn_a=w_ffn_a, w_ffn_b=w_ffn_b,
                     ffn_conv_w=ffn_conv_w, ffn_conv_b=ffn_conv_b, w_ffn_down=w_ffn_down)
    depth = w_in.shape[0]
    bp, t_p, _ = x_prompt.shape
    bs, t_s, _ = x_sample.shape
    n_mem = mem_prompt.shape[1]
    cfg_p = dict(_PROMPT_CFG, bsz=bp, t_len=t_p)
    cfg_s = dict(_SAMPLE_CFG, bsz=bs, t_len=t_s)
    ln_f = ln_f_g[None]

    xp = x_prompt.reshape(bp * t_p, D_MODEL)
    xs = x_sample.reshape(bs * t_s, D_MODEL)
    mem_flat = mem_prompt.reshape(bp * n_mem, D_MODEL)
    p_states, s_states = [], []
    for l in range(depth):
        w = _pack_layer({k: v[l] for k, v in per_layer.items()})
        ssm = _s5_prep(w["a_re"], w["a_im"], w["log_dt"], w["b_re_blk"], w["b_im_blk"])
        last = l == depth - 1

        mem_k, mem_v = _mem_proj(mem_flat, w["ln_mem_g"], w["w_xk"], w["w_xv"], 512)
        mem_k = mem_k.reshape(bp, n_mem, X_HEADS, X_DH)
        mem_v = mem_v.reshape(bp, n_mem, X_HEADS, X_DH)
        xp, st = _layer(
            xp, w, ssm, mem_k, mem_v, None, None,
            jnp.zeros((bp, ML_HEADS, ML_DH, ML_DH), F32), jnp.zeros((bp, ML_HEADS, ML_DH), F32),
            jnp.zeros((bp, ML_HEADS), F32),
            jnp.zeros((bp, SSM_GROUPS, SSM_P), F32), jnp.zeros((bp, SSM_GROUPS, SSM_P), F32),
            jnp.zeros((bp, CONV_W - 1, D_FF), F32), ln_f, cfg_p, last)
        p_states.append(st + (mem_k, mem_v))

        xs, st = _layer(
            xs, w, ssm, cache_mem_k[l], cache_mem_v[l], cache_sb_k[l], cache_sb_v[l],
            state_mlstm_c[l], state_mlstm_n[l], state_mlstm_m[l],
            state_ssm_re[l], state_ssm_im[l], state_ffn_conv[l], ln_f, cfg_s, last)
        s_states.append(st)

    p_out = [jnp.stack(a) for a in zip(*p_states)]
    s_out = [jnp.stack(a) for a in zip(*s_states)]
    return (xp.reshape(bp, t_p, D_MODEL), xs.reshape(bs, t_s, D_MODEL), *p_out, *s_out)
```

```python
import functools
import math

import jax
import jax.numpy as jnp
from jax import lax
from jax.experimental import pallas as pl
from jax.experimental.pallas import tpu as pltpu

F32 = jnp.float32
BF16 = jnp.bfloat16

D_MODEL = 1024
EPS = 1e-6
CHUNK = 64
ML_HEADS = 4
ML_DH = 96
ML_DHP = 128
ML_W = ML_HEADS * ML_DH
ML_WP = ML_HEADS * ML_DHP
SB_DH = 64
SB_HEADS = 6
SB_W = SB_HEADS * SB_DH
SB_PAIRS = SB_HEADS // 2
SSM_W = 256
SSM_CH = 16
SSM_GROUPS = 16
SSM_P = 64
SSM_N = SSM_GROUPS * SSM_P
X_HEADS = 4
X_DH = 256
N_MEM = 256
D_FF = 2816
CONV_W = 3
FF_CHUNK = 256
KEY_BLOCK = 256
LANE = 128
VMEM_LIMIT = 56 * 1024 * 1024

_OFF_QB = 4 * ML_WP
_OFF_KB = _OFF_QB + SB_W
_OFF_VB = _OFF_KB + SB_W
_OFF_UC = _OFF_VB + SB_W
_N_MAIN = _OFF_UC + SSM_W


def _cparams(*sem):
    return pltpu.CompilerParams(dimension_semantics=sem, vmem_limit_bytes=VMEM_LIMIT)


def _const_spec(shape):
    nd = len(shape)
    return pl.BlockSpec(shape, lambda *_: (0,) * nd, pipeline_mode=pl.Buffered(1))


def _rmsnorm(x, g):
    ms = jnp.mean(x * x, axis=-1, keepdims=True)
    return x * lax.rsqrt(ms + EPS) * g


def _sigmoid(x):
    return 1.0 / (1.0 + jnp.exp(-x))


def _log_sigmoid(x):
    return jnp.minimum(x, 0.0) - jnp.log(1.0 + jnp.exp(-jnp.abs(x)))


def _dot(a, b):
    return jnp.dot(a, b, preferred_element_type=F32)


def _dot_nt(a, b):
    return lax.dot_general(a, b, (((1,), (1,)), ((), ())), preferred_element_type=F32)


def _dot_tn(a, b):
    return lax.dot_general(a, b, (((0,), (0,)), ((), ())), preferred_element_type=F32)


def _split_dot(x, m_bf16, terms):
    acc = None
    r = x
    for i in range(terms):
        p = r.astype(BF16)
        d = _dot(p, m_bf16)
        acc = d if acc is None else acc + d
        if i + 1 < terms:
            r = r - p.astype(F32)
    return acc


def _in_proj_body(x_ref, g_ref, w_ref, b_ref, wg_ref, bg_ref,
                  za_ref, qb_ref, kb_ref, vb_ref, uc_ref, gt_ref):
    h = _rmsnorm(x_ref[...], g_ref[...]).astype(BF16)

    def proj(lo, hi):
        return _dot(h, w_ref[:, lo:hi]) + b_ref[:, lo:hi]

    za_ref[...] = proj(0, _OFF_QB)
    qb_ref[...] = proj(_OFF_QB, _OFF_KB)
    kb_ref[...] = proj(_OFF_KB, _OFF_VB)
    vb_ref[...] = proj(_OFF_VB, _OFF_UC)
    uc_ref[...] = proj(_OFF_UC, _N_MAIN)
    gt_ref[0] = _dot_nt(wg_ref[...], h) + bg_ref[...]


def _in_proj(x, g, w, b, wg, bg, bsz, t_len, tm):
    n = bsz * t_len
    nt = t_len // tm
    row = lambda width: pl.BlockSpec((tm, width), lambda bi, ti: (bi * nt + ti, 0))
    return pl.pallas_call(
        _in_proj_body,
        grid=(bsz, nt),
        in_specs=[row(D_MODEL), _const_spec((1, D_MODEL)), _const_spec((D_MODEL, _N_MAIN)),
                  _const_spec((1, _N_MAIN)), _const_spec((8, D_MODEL)), _const_spec((8, 1))],
        out_specs=[row(4 * ML_WP), row(SB_W), row(SB_W), row(SB_W),
                   pl.BlockSpec((tm, SSM_W), lambda bi, ti: (ti, bi)),
                   pl.BlockSpec((1, 8, tm), lambda bi, ti: (bi * nt + ti, 0, 0))],
        out_shape=[jax.ShapeDtypeStruct((n, 4 * ML_WP), F32),
                   jax.ShapeDtypeStruct((n, SB_W), F32),
                   jax.ShapeDtypeStruct((n, SB_W), F32),
                   jax.ShapeDtypeStruct((n, SB_W), F32),
                   jax.ShapeDtypeStruct((t_len, bsz * SSM_W), F32),
                   jax.ShapeDtypeStruct((bsz * nt, 8, tm), F32)],
        compiler_params=_cparams("parallel", "parallel"),
        name="in_proj",
    )(x, g, w, b, wg, bg)


def _mlstm_body(blk, q_ref, k_ref, v_ref, o_ref, g_ref, c0_ref, n0_ref, m0_ref, gn_ref,
                ya_ref, c_ref, n_ref, m_ref):
    @pl.when(pl.program_id(1) == 0)
    def _():
        c_ref[...] = c0_ref[...]
        n_ref[...] = n0_ref[...]
        m_ref[...] = m0_ref[...]

    ts = q_ref.shape[0]
    t_i = lax.broadcasted_iota(jnp.int32, (blk, blk), 0)
    s_i = lax.broadcasted_iota(jnp.int32, (blk, blk), 1)
    causal = s_i <= t_i
    eye = s_i == t_i
    upper = jnp.where(t_i <= s_i, 1.0, 0.0).astype(BF16)
    k_scale = ML_DH ** -0.5

    for ci in range(ts // blk):
        r0 = ci * blk
        g = g_ref[0, :, r0:r0 + blk]
        lf_all = _log_sigmoid(g)
        b_rows = _split_dot(lf_all, upper, 3)
        for h in range(ML_HEADS):
            c0, c1 = h * ML_DHP, (h + 1) * ML_DHP
            q = q_ref[r0:r0 + blk, c0:c1]
            k = k_ref[r0:r0 + blk, c0:c1] * k_scale
            v = v_ref[r0:r0 + blk, c0:c1]
            qb, kb, vb = q.astype(BF16), k.astype(BF16), v.astype(BF16)
            li = g[h:h + 1, :]
            lf = lf_all[4 + h:5 + h, :]
            a_row = li - b_rows[4 + h:5 + h, :]
            b_col = jnp.sum(jnp.where(causal, lf, 0.0), axis=-1, keepdims=True)
            m_prev = m_ref[0, h][:, 0:1]
            n_row = n_ref[0, h]
            ct = c_ref[0, h]

            a_m = jnp.where(causal, a_row, -jnp.inf)
            m_col = jnp.maximum(jnp.max(a_m, axis=-1, keepdims=True), m_prev)
            w_intra = jnp.exp(a_m - m_col)
            w_inter = jnp.exp(m_prev - m_col)
            s = _dot_nt(qb, kb) * w_intra
            num = _dot(s.astype(BF16), vb) + w_inter * _dot(qb, ct.astype(BF16))
            den = (jnp.sum(s, axis=-1, keepdims=True)
                   + w_inter * jnp.sum(q * n_row, axis=-1, keepdims=True))
            hh = num / jnp.maximum(jnp.abs(den), jnp.exp(-(b_col + m_col)))

            x = _sigmoid(o_ref[r0:r0 + blk, c0:c1]) * hh
            ms = jnp.sum(x * x, axis=-1, keepdims=True) * (1.0 / ML_DH)
            ya_ref[r0:r0 + blk, c0:c1] = x * lax.rsqrt(ms + EPS) * gn_ref[:, c0:c1]

            m_last = m_col[blk - 1:blk, :]
            wg_row = jnp.exp(a_row - m_last)
            wg_col = jnp.sum(jnp.where(eye, wg_row, 0.0), axis=-1, keepdims=True)
            decay = jnp.exp(m_prev - m_last)
            c_ref[0, h] = decay * ct + _dot_tn((k * wg_col).astype(BF16), vb)
            wg8 = jnp.broadcast_to(wg_row, (8, blk)).astype(BF16)
            n_ref[0, h] = decay * n_row + _dot(wg8, kb)[0:1, :]
            m_ref[0, h] = jnp.broadcast_to(b_col[blk - 1:blk, :] + m_last, (1, LANE))


def _mlstm(za, o_src, gates, c0, n0, m0, gn, bsz, t_len, ts, blk):
    n = bsz * t_len
    ns = t_len // ts
    col = lambda j: pl.BlockSpec((ts, ML_WP), lambda bi, si: (bi * ns + si, j))
    st4 = pl.BlockSpec((1, ML_HEADS, ML_DHP, ML_DHP), lambda bi, si: (bi, 0, 0, 0))
    st3 = pl.BlockSpec((1, ML_HEADS, 1, LANE), lambda bi, si: (bi, 0, 0, 0))
    return pl.pallas_call(
        functools.partial(_mlstm_body, blk),
        grid=(bsz, ns),
        in_specs=[col(0), col(1), col(2), col(3),
                  pl.BlockSpec((1, 8, ts), lambda bi, si: (bi * ns + si, 0, 0)),
                  st4, st3, st3, _const_spec((1, ML_WP))],
        out_specs=[pl.BlockSpec((ts, ML_WP), lambda bi, si: (bi * ns + si, 0)), st4, st3, st3],
        out_shape=[jax.ShapeDtypeStruct((n, ML_WP), F32),
                   jax.ShapeDtypeStruct((bsz, ML_HEADS, ML_DHP, ML_DHP), F32),
                   jax.ShapeDtypeStruct((bsz, ML_HEADS, 1, LANE), F32),
                   jax.ShapeDtypeStruct((bsz, ML_HEADS, 1, LANE), F32)],
        compiler_params=_cparams("parallel", "arbitrary"),
        name="mlstm",
    )(za, za, za, o_src, gates, c0, n0, m0, gn)


def _sb_body(past_off, n_q, q_ref, kc_ref, vc_ref, kp_ref, vp_ref, gn_ref, y_ref, acc_ref, car_ref):
    tq = q_ref.shape[1]
    qi = pl.program_id(1)
    lane = lax.broadcasted_iota(jnp.int32, (1, LANE), 1)
    first = lane < SB_DH
    q = q_ref[0] * (SB_DH ** -0.5)
    qh = []
    for p in range(SB_PAIRS):
        qp = q[:, p * LANE:(p + 1) * LANE]
        qh += [jnp.where(first, qp, 0.0).astype(BF16), jnp.where(first, 0.0, qp).astype(BF16)]

    def ones_and_after(kb):
        r = lax.broadcasted_iota(jnp.int32, (kb, LANE + kb), 0)
        c = lax.broadcasted_iota(jnp.int32, (kb, LANE + kb), 1)
        return jnp.where((c < LANE) | (r > c - LANE), 1.0, 0.0).astype(BF16)

    def block(k, v, uo, vis):
        kb = k.shape[0]
        for p in range(SB_PAIRS):
            kbf = k[:, p * LANE:(p + 1) * LANE].astype(BF16)
            vbf = v[:, p * LANE:(p + 1) * LANE].astype(BF16)
            for hd in (2 * p, 2 * p + 1):
                z = _dot_nt(qh[hd], kbf)
                l1m = -(jnp.maximum(z, 0.0) + jnp.log(1.0 + jnp.exp(-jnp.abs(z))))
                lmask = l1m if vis is None else jnp.where(vis, l1m, 0.0)
                cs = _split_dot(lmask, uo, 2)
                if vis is None:
                    car = car_ref[hd]
                    car = car[:, :kb] if kb <= LANE else jnp.concatenate([car] * (kb // LANE), axis=1)
                    after = car + cs[:, LANE:]
                    a = jnp.exp(z + l1m + after)
                    acc_ref[hd] += _dot(a.astype(BF16), vbf)
                    car_ref[hd] += cs[:, :LANE]
                else:
                    a = jnp.where(vis, jnp.exp(z + l1m + cs[:, LANE:]), 0.0)
                    acc_ref[hd] = _dot(a.astype(BF16), vbf)
                    car_ref[hd] = cs[:, :LANE]

    uo_cur = ones_and_after(tq)
    uo_past = uo_cur if tq == KEY_BLOCK else ones_and_after(KEY_BLOCK)
    t_i = lax.broadcasted_iota(jnp.int32, (tq, tq), 0)
    s_i = lax.broadcasted_iota(jnp.int32, (tq, tq), 1)
    block(kc_ref[0], vc_ref[0], uo_cur, s_i < t_i)

    if n_q == 1:
        for j in reversed(range(past_off // KEY_BLOCK)):
            block(kp_ref[0, j * KEY_BLOCK:(j + 1) * KEY_BLOCK, :], vp_ref[0, j * KEY_BLOCK:(j + 1) * KEY_BLOCK, :],
                  uo_past, None)
    else:
        n_past = (past_off + qi * tq) // KEY_BLOCK

        def past(j, carry):
            start = pl.multiple_of((n_past - 1 - j) * KEY_BLOCK, KEY_BLOCK)
            block(kp_ref[0, pl.ds(start, KEY_BLOCK), :], vp_ref[0, pl.ds(start, KEY_BLOCK), :], uo_past, None)
            return carry

        lax.fori_loop(0, n_past, past, 0)

    for p in range(SB_PAIRS):
        o = jnp.where(first, acc_ref[2 * p], acc_ref[2 * p + 1])
        o2 = o * o
        ss0 = jnp.sum(jnp.where(first, o2, 0.0), axis=-1, keepdims=True)
        ss1 = jnp.sum(jnp.where(first, 0.0, o2), axis=-1, keepdims=True)
        inv = jnp.where(first, lax.rsqrt(ss0 * (1.0 / SB_DH) + EPS), lax.rsqrt(ss1 * (1.0 / SB_DH) + EPS))
        y_ref[0, :, p * LANE:(p + 1) * LANE] = o * inv * gn_ref[:, p * LANE:(p + 1) * LANE]


def _sb_attn(q, kc, vc, kp, vp, gn, tq, past_off):
    bsz, t_len, _ = q.shape
    p_len = kp.shape[1]
    n_q = t_len // tq
    assert (past_off % KEY_BLOCK == 0) and (n_q == 1 or tq % KEY_BLOCK == 0)
    cur = pl.BlockSpec((1, tq, SB_W), lambda bi, qi: (bi, qi, 0))
    past = pl.BlockSpec((1, p_len, SB_W), lambda bi, qi: (bi, 0, 0))
    return pl.pallas_call(
        functools.partial(_sb_body, past_off, n_q),
        grid=(bsz, n_q),
        in_specs=[cur, cur, cur, past, past, _const_spec((1, SB_W))],
        out_specs=cur,
        out_shape=jax.ShapeDtypeStruct((bsz, t_len, SB_W), F32),
        scratch_shapes=[pltpu.VMEM((SB_HEADS, tq, LANE), F32), pltpu.VMEM((SB_HEADS, tq, LANE), F32)],
        compiler_params=_cparams("parallel", "arbitrary"),
        name="sb_attn",
    )(q, kc, vc, kp, vp, gn)


def _s5_prep_body(are_ref, aim_ref, ldt_ref, bre_ref, bim_ref, ab_ref, bcat_ref):
    a_re, a_im = are_ref[...], aim_ref[...]
    dt = jnp.exp(ldt_ref[...])
    mag = jnp.exp(a_re * dt)
    ab_re = mag * jnp.cos(a_im * dt)
    ab_im = mag * jnp.sin(a_im * dt)
    den = a_re * a_re + a_im * a_im
    nr = ab_re - 1.0
    zr = (nr * a_re + ab_im * a_im) / den
    zi = (ab_im * a_re - nr * a_im) / den
    ab_ref[0:1, :] = ab_re
    ab_ref[1:2, :] = ab_im
    b_re, b_im = bre_ref[...], bim_ref[...]
    bcat_ref[:, :SSM_N] = (zr * b_re - zi * b_im).astype(BF16)
    bcat_ref[:, SSM_N:] = (zr * b_im + zi * b_re).astype(BF16)


def _s5_prep(a_re, a_im, log_dt, b_re_blk, b_im_blk):
    return pl.pallas_call(
        _s5_prep_body,
        out_shape=[jax.ShapeDtypeStruct((2, SSM_N), F32), jax.ShapeDtypeStruct((SSM_W, 2 * SSM_N), BF16)],
        name="s5_prep",
    )(a_re, a_im, log_dt, b_re_blk, b_im_blk)


def _gelu_tanh(x):
    return 0.5 * x * (1.0 + jnp.tanh(math.sqrt(2.0 / math.pi) * (x + 0.044715 * (x * x * x))))


def _s5_body(u_ref, bcat_ref, ab_ref, cre_ref, cim_ref, d_ref, wglu_ref, bglu_ref, gn_ref,
             h0r_ref, h0i_ref, y_ref, hr_ref, hi_ref, xs_ref):
    tt, bsz, _ = u_ref.shape

    @pl.when(pl.program_id(0) == 0)
    def _():
        hr_ref[...] = h0r_ref[...]
        hi_ref[...] = h0i_ref[...]

    u2 = u_ref[...].reshape(tt * bsz, SSM_W)
    xs_ref[...] = _dot(u2.astype(BF16), bcat_ref[...])
    ar = jnp.broadcast_to(ab_ref[0:1, :], (bsz, SSM_N))
    ai = jnp.broadcast_to(ab_ref[1:2, :], (bsz, SSM_N))

    def step(t, carry):
        hr, hi = carry
        off = pl.multiple_of(t * bsz, bsz)
        xr = xs_ref[pl.ds(off, bsz), :SSM_N]
        xi = xs_ref[pl.ds(off, bsz), SSM_N:]
        nr = ar * hr - ai * hi + xr
        ni = ar * hi + ai * hr + xi
        xs_ref[pl.ds(off, bsz), :SSM_N] = nr
        xs_ref[pl.ds(off, bsz), SSM_N:] = ni
        return nr, ni

    hr, hi = lax.fori_loop(0, tt, step, (hr_ref[...], hi_ref[...]))
    hr_ref[...] = hr
    hi_ref[...] = hi

    y = (_dot(xs_ref[:, :SSM_N].astype(BF16), cre_ref[...])
         - _dot(xs_ref[:, SSM_N:].astype(BF16), cim_ref[...])
         + d_ref[...] * u2)
    g = _gelu_tanh(y)
    gate = _sigmoid(_dot(g.astype(BF16), wglu_ref[...]) + bglu_ref[...])
    y_ref[...] = _rmsnorm(g * gate, gn_ref[...]).reshape(tt, bsz, SSM_W)


def _s5(u_tm, bcat, ab, c_re, c_im, d, w_glu, b_glu, gn, h0r, h0i, tt):
    t_len, bsz, _ = u_tm.shape
    tile = pl.BlockSpec((tt, bsz, SSM_W), lambda ti: (ti, 0, 0))
    st = pl.BlockSpec((bsz, SSM_N), lambda ti: (0, 0))
    return pl.pallas_call(
        _s5_body,
        grid=(t_len // tt,),
        in_specs=[tile, _const_spec((SSM_W, 2 * SSM_N)), _const_spec((2, SSM_N)),
                  _const_spec((SSM_N, SSM_W)), _const_spec((SSM_N, SSM_W)), _const_spec((1, SSM_W)),
                  _const_spec((SSM_W, SSM_W)), _const_spec((1, SSM_W)), _const_spec((1, SSM_W)), st, st],
        out_specs=[tile, st, st],
        out_shape=[jax.ShapeDtypeStruct((t_len, bsz, SSM_W), F32),
                   jax.ShapeDtypeStruct((bsz, SSM_N), F32),
                   jax.ShapeDtypeStruct((bsz, SSM_N), F32)],
        scratch_shapes=[pltpu.VMEM((tt * bsz, 2 * SSM_N), F32)],
        compiler_params=_cparams("arbitrary"),
        name="s5",
    )(u_tm, bcat, ab, c_re, c_im, d, w_glu, b_glu, gn, h0r, h0i)


def _out_proj_body(x_ref, ya_ref, yb_ref, yc_ref, woa_ref, wob_ref, woc_ref, lnx_ref, wxq_ref,
                   x1_ref, qx_ref):
    x1 = (x_ref[...]
          + _dot(ya_ref[...].astype(BF16), woa_ref[...])
          + _dot(yb_ref[...].astype(BF16), wob_ref[...])
          + _dot(yc_ref[...].astype(BF16), woc_ref[...]))
    x1_ref[...] = x1
    hq = _rmsnorm(x1, lnx_ref[...]).astype(BF16)
    qx_ref[...] = _dot(hq, wxq_ref[...]).astype(BF16)


def _out_proj(x, ya, yb, yc_tm, woa, wob, woc, lnx, wxq, bsz, t_len, tm):
    n = bsz * t_len
    nt = t_len // tm
    row = lambda width: pl.BlockSpec((tm, width), lambda bi, ti: (bi * nt + ti, 0))
    return pl.pallas_call(
        _out_proj_body,
        grid=(bsz, nt),
        in_specs=[row(D_MODEL), row(ML_WP), row(SB_W),
                  pl.BlockSpec((tm, SSM_W), lambda bi, ti: (ti, bi)),
                  _const_spec((ML_WP, D_MODEL)), _const_spec((SB_W, D_MODEL)), _const_spec((SSM_W, D_MODEL)),
                  _const_spec((1, D_MODEL)), _const_spec((D_MODEL, D_MODEL))],
        out_specs=[row(D_MODEL), row(D_MODEL)],
        out_shape=[jax.ShapeDtypeStruct((n, D_MODEL), F32), jax.ShapeDtypeStruct((n, D_MODEL), BF16)],
        compiler_params=_cparams("parallel", "parallel"),
        name="out_proj",
    )(x, ya, yb, yc_tm, woa, wob, woc, lnx, wxq)


def _mem_proj_body(m_ref, g_ref, wk_ref, wv_ref, k_ref, v_ref):
    mn = _rmsnorm(m_ref[...], g_ref[...]).astype(BF16)
    k_ref[...] = _dot(mn, wk_ref[...])
    v_ref[...] = _dot(mn, wv_ref[...])


def _mem_proj(mem, g, wk, wv, tm):
    n = mem.shape[0]
    row = pl.BlockSpec((tm, D_MODEL), lambda i: (i, 0))
    return pl.pallas_call(
        _mem_proj_body,
        grid=(n // tm,),
        in_specs=[row, _const_spec((1, D_MODEL)), _const_spec((D_MODEL, D_MODEL)), _const_spec((D_MODEL, D_MODEL))],
        out_specs=[row, row],
        out_shape=[jax.ShapeDtypeStruct((n, D_MODEL), F32)] * 2,
        compiler_params=_cparams("parallel"),
        name="mem_proj",
    )(mem, g, wk, wv)


def _x_attn_body(x_ref, q_ref, k_ref, v_ref, wo_ref, o_ref):
    acc = x_ref[...]
    for h in range(X_HEADS):
        c0, c1 = h * X_DH, (h + 1) * X_DH
        kh = k_ref[0, :, c0:c1].astype(BF16)
        vh = v_ref[0, :, c0:c1].astype(BF16)
        s = _dot_nt(q_ref[:, c0:c1], kh) * (X_DH ** -0.5)
        e = jnp.exp(s - jnp.max(s, axis=-1, keepdims=True))
        p = e / jnp.sum(e, axis=-1, keepdims=True)
        oh = _dot(p.astype(BF16), vh)
        acc = acc + _dot(oh.astype(BF16), wo_ref[c0:c1, :])
    o_ref[...] = acc


def _x_attn(x1, qx, mem_k, mem_v, wo, bsz, t_len, tq):
    n = bsz * t_len
    nt = t_len // tq
    row = pl.BlockSpec((tq, D_MODEL), lambda bi, ti: (bi * nt + ti, 0))
    mem = pl.BlockSpec((1, N_MEM, D_MODEL), lambda bi, ti: (bi, 0, 0))
    return pl.pallas_call(
        _x_attn_body,
        grid=(bsz, nt),
        in_specs=[row, row, mem, mem, _const_spec((D_MODEL, D_MODEL))],
        out_specs=row,
        out_shape=jax.ShapeDtypeStruct((n, D_MODEL), F32),
        compiler_params=_cparams("parallel", "parallel"),
        name="x_attn",
    )(x1, qx, mem_k, mem_v, wo)


def _conv_ffn_body(final_norm, x_ref, g_ref, wa_ref, wb_ref, cw_ref, cb_ref, wd_ref, prev_ref, gf_ref,
                   o_ref, conv_ref, abuf_ref, acc_ref):
    tm = x_ref.shape[0]

    @pl.when(pl.program_id(1) == 0)
    def _():
        conv_ref[...] = prev_ref[...]

    x = x_ref[...]
    hb = _rmsnorm(x, g_ref[...]).astype(BF16)
    acc_ref[...] = x
    for c in range(D_FF // FF_CHUNK):
        c0, c1 = c * FF_CHUNK, (c + 1) * FF_CHUNK
        a = _dot(hb, wa_ref[:, c0:c1])
        abuf_ref[6:8, :] = conv_ref[0, :, c0:c1]
        abuf_ref[8:8 + tm, :] = a
        conv_ref[0, :, c0:c1] = a[tm - 2:tm, :]
        cv = (cb_ref[:, c0:c1]
              + abuf_ref[6:6 + tm, :] * cw_ref[0:1, c0:c1]
              + abuf_ref[7:7 + tm, :] * cw_ref[1:2, c0:c1]
              + a * cw_ref[2:3, c0:c1])
        y = cv * _sigmoid(cv) * _dot(hb, wb_ref[:, c0:c1])
        acc_ref[...] += _dot(y.astype(BF16), wd_ref[c0:c1, :])
    out = acc_ref[...]
    if final_norm:
        out = _rmsnorm(out, gf_ref[...])
    o_ref[...] = out


def _conv_ffn(x, g, wa, wb, cw, cb, wd, prev, gf, bsz, t_len, tm, final_norm):
    n = bsz * t_len
    nt = t_len // tm
    row = pl.BlockSpec((tm, D_MODEL), lambda bi, ti: (bi * nt + ti, 0))
    conv = pl.BlockSpec((1, CONV_W - 1, D_FF), lambda bi, ti: (bi, 0, 0))
    return pl.pallas_call(
        functools.partial(_conv_ffn_body, final_norm),
        grid=(bsz, nt),
        in_specs=[row, _const_spec((1, D_MODEL)), _const_spec((D_MODEL, D_FF)), _const_spec((D_MODEL, D_FF)),
                  _const_spec((CONV_W, D_FF)), _const_spec((1, D_FF)), _const_spec((D_FF, D_MODEL)),
                  conv, _const_spec((1, D_MODEL))],
        out_specs=[row, conv],
        out_shape=[jax.ShapeDtypeStruct((n, D_MODEL), F32),
                   jax.ShapeDtypeStruct((bsz, CONV_W - 1, D_FF), F32)],
        scratch_shapes=[pltpu.VMEM((tm + 8, FF_CHUNK), F32), pltpu.VMEM((tm, D_MODEL), F32)],
        compiler_params=_cparams("parallel", "arbitrary"),
        name="conv_ffn",
    )(x, g, wa, wb, cw, cb, wd, prev, gf)


def _pad_heads(a, axis):
    shp = a.shape
    a = a.reshape(shp[:axis] + (ML_HEADS, ML_DH) + shp[axis + 1:])
    pad = [(0, 0)] * a.ndim
    pad[axis + 1] = (0, ML_DHP - ML_DH)
    a = jnp.pad(a, pad)
    return a.reshape(shp[:axis] + (ML_WP,) + shp[axis + 1:])


def _block_diag(blocks):
    g, r, c = blocks.shape
    eye = jnp.eye(g, dtype=blocks.dtype)
    return (eye[:, None, :, None] * blocks[:, :, None, :]).reshape(g * r, g * c)


def _pack_layer(p):
    w_in, b_in = p["w_in"], p["b_in"]
    o = 0
    parts_w, parts_b = [], []
    for _ in range(4):
        parts_w.append(_pad_heads(w_in[:, o:o + ML_W], 1))
        parts_b.append(_pad_heads(b_in[o:o + ML_W], 0))
        o += ML_W
    wg = w_in[:, o:o + 2 * ML_HEADS].T
    bg = b_in[o:o + 2 * ML_HEADS][:, None]
    o += 2 * ML_HEADS
    parts_w.append(w_in[:, o:])
    parts_b.append(b_in[o:])
    w_out = p["w_out"]
    return dict(
        ln_mix_g=p["ln_mix_g"][None], w_main=jnp.concatenate(parts_w, axis=1).astype(BF16),
        b_main=jnp.concatenate(parts_b)[None], wg=wg.astype(BF16), bg=bg,
        gn_a=_pad_heads(p["gn_a_g"], 0)[None], gn_b=p["gn_b_g"][None], gn_c=p["gn_c_g"][None],
        a_re=p["ssm_a_re"].reshape(1, SSM_N), a_im=p["ssm_a_im"].reshape(1, SSM_N),
        log_dt=p["ssm_log_dt"].reshape(1, SSM_N),
        b_re_blk=_block_diag(jnp.swapaxes(p["ssm_b_re"], 1, 2)),
        b_im_blk=_block_diag(jnp.swapaxes(p["ssm_b_im"], 1, 2)),
        c_re_blk=_block_diag(jnp.swapaxes(p["ssm_c_re"], 1, 2)).astype(BF16),
        c_im_blk=_block_diag(jnp.swapaxes(p["ssm_c_im"], 1, 2)).astype(BF16),
        ssm_d=p["ssm_d"][None], w_glu=p["w_glu"].astype(BF16), b_glu=p["b_glu"][None],
        w_out_a=_pad_heads(w_out[:ML_W], 0).astype(BF16), w_out_b=w_out[ML_W:ML_W + SB_W].astype(BF16),
        w_out_c=w_out[ML_W + SB_W:].astype(BF16),
        ln_x_g=p["ln_x_g"][None], w_xq=p["w_xq"].astype(BF16), w_xo=p["w_xo"].astype(BF16),
        ln_mem_g=p["ln_mem_g"][None], w_xk=p["w_xk"].astype(BF16), w_xv=p["w_xv"].astype(BF16),
        ln_ffn_g=p["ln_ffn_g"][None], w_ffn_a=p["w_ffn_a"].astype(BF16), w_ffn_b=p["w_ffn_b"].astype(BF16),
        ffn_conv_w=p["ffn_conv_w"], ffn_conv_b=p["ffn_conv_b"][None], w_ffn_down=p["w_ffn_down"].astype(BF16),
    )


def _layer(x, w, ssm, mem_k, mem_v, past_k, past_v, c0, n0, m0, s0_re, s0_im, conv_prev, ln_f, cfg, final_norm):
    bsz, t_len = cfg["bsz"], cfg["t_len"]
    tm, ts, blk = cfg["tm"], cfg["ts"], cfg["blk"]
    ab, bcat = ssm

    za, qb, kb, vb, u_tm, gates = _in_proj(x, w["ln_mix_g"], w["w_main"], w["b_main"], w["wg"], w["bg"], bsz, t_len, tm)
    if tm != ts:
        gates = gates.reshape(-1, 8, tm // ts, ts).transpose(0, 2, 1, 3).reshape(-1, 8, ts)

    ct0 = jnp.pad(jnp.swapaxes(c0, -1, -2), ((0, 0), (0, 0), (0, ML_DHP - ML_DH), (0, ML_DHP - ML_DH)))
    n0p = jnp.pad(n0, ((0, 0), (0, 0), (0, ML_DHP - ML_DH)))[:, :, None, :]
    m0p = jnp.broadcast_to(m0[:, :, None, None], (bsz, ML_HEADS, 1, LANE))
    ya, ct, n_new, m_new = _mlstm(za, za, gates, ct0, n0p, m0p, w["gn_a"], bsz, t_len, ts, blk)

    q3, k3, v3 = (a.reshape(bsz, t_len, SB_W) for a in (qb, kb, vb))
    if past_k is None:
        yb = _sb_attn(q3, k3, v3, k3, v3, w["gn_b"], cfg["tq_sb"], 0)
    else:
        p_len = past_k.shape[1]
        yb = _sb_attn(q3, k3, v3, past_k.reshape(bsz, p_len, SB_W), past_v.reshape(bsz, p_len, SB_W),
                      w["gn_b"], cfg["tq_sb"], p_len)
    yb = yb.reshape(bsz * t_len, SB_W)

    yc_tm, s_re, s_im = _s5(u_tm.reshape(t_len, bsz, SSM_W), bcat, ab, w["c_re_blk"], w["c_im_blk"], w["ssm_d"],
                            w["w_glu"], w["b_glu"], w["gn_c"], s0_re.reshape(bsz, SSM_N), s0_im.reshape(bsz, SSM_N),
                            cfg["tt"])

    x1, qx = _out_proj(x, ya, yb, yc_tm.reshape(t_len, bsz * SSM_W), w["w_out_a"], w["w_out_b"], w["w_out_c"],
                       w["ln_x_g"], w["w_xq"], bsz, t_len, tm)
    x2 = _x_attn(x1, qx, mem_k.reshape(bsz, N_MEM, D_MODEL), mem_v.reshape(bsz, N_MEM, D_MODEL), w["w_xo"],
                 bsz, t_len, tm)
    x3, conv_new = _conv_ffn(x2, w["ln_ffn_g"], w["w_ffn_a"], w["w_ffn_b"], w["ffn_conv_w"], w["ffn_conv_b"],
                             w["w_ffn_down"], conv_prev, ln_f, bsz, t_len, cfg["tm_ffn"], final_norm)

    states = (kb.reshape(bsz, t_len, SB_HEADS, SB_DH), vb.reshape(bsz, t_len, SB_HEADS, SB_DH),
              jnp.swapaxes(ct[:, :, :ML_DH, :ML_DH], -1, -2), n_new[:, :, 0, :ML_DH], m_new[:, :, 0, 0],
              s_re.reshape(bsz, SSM_GROUPS, SSM_P), s_im.reshape(bsz, SSM_GROUPS, SSM_P), conv_new)
    return x3, states


_PROMPT_CFG = dict(tm=512, ts=256, blk=CHUNK, tq_sb=256, tt=64, tm_ffn=256)
_SAMPLE_CFG = dict(tm=32, ts=32, blk=32, tq_sb=32, tt=32, tm_ffn=32)


def kernel(x_prompt, x_sample, cache_sb_k, cache_sb_v, state_mlstm_c, state_mlstm_n, state_mlstm_m, state_ssm_re, state_ssm_im, state_ffn_conv, cache_mem_k, cache_mem_v, mem_prompt, ln_mix_g, w_in, b_in, gn_a_g, gn_b_g, gn_c_g, ssm_a_re, ssm_a_im, ssm_log_dt, ssm_b_re, ssm_b_im, ssm_c_re, ssm_c_im, ssm_d, w_glu, b_glu, w_out, ln_x_g, ln_mem_g, w_xq, w_xk, w_xv, w_xo, ln_ffn_g, w_ffn_a, w_ffn_b, ffn_conv_w, ffn_conv_b, w_ffn_down, ln_f_g):
    per_layer = dict(ln_mix_g=ln_mix_g, w_in=w_in, b_in=b_in, gn_a_g=gn_a_g, gn_b_g=gn_b_g, gn_c_g=gn_c_g,
                     ssm_a_re=ssm_a_re, ssm_a_im=ssm_a_im, ssm_log_dt=ssm_log_dt, ssm_b_re=ssm_b_re,
                     ssm_b_im=ssm_b_im, ssm_c_re=ssm_c_re, ssm_c_im=ssm_c_im, ssm_d=ssm_d, w_glu=w_glu,
                     b_glu=b_glu, w_out=w_out, ln_x_g=ln_x_g, ln_mem_g=ln_mem_g, w_xq=w_xq, w_xk=w_xk,
                     w_xv=w_xv, w_xo=w_xo, ln_ffn_g=ln_ffn_g, w_ffn_a=w_ffn_a, w_ffn_b=w_ffn_b,
                     ffn_conv_w=ffn_conv_w, ffn_conv_b=ffn_conv_b, w_ffn_down=w_ffn_down)
    depth = w_in.shape[0]
    bp, t_p, _ = x_prompt.shape
    bs, t_s, _ = x_sample.shape
    n_mem = mem_prompt.shape[1]
    cfg_p = dict(_PROMPT_CFG, bsz=bp, t_len=t_p)
    cfg_s = dict(_SAMPLE_CFG, bsz=bs, t_len=t_s)
    ln_f = ln_f_g[None]

    xp = x_prompt.reshape(bp * t_p, D_MODEL)
    xs = x_sample.reshape(bs * t_s, D_MODEL)
    mem_flat = mem_prompt.reshape(bp * n_mem, D_MODEL)
    p_states, s_states = [], []
    for l in range(depth):
        w = _pack_layer({k: v[l] for k, v in per_layer.items()})
        ssm = _s5_prep(w["a_re"], w["a_im"], w["log_dt"], w["b_re_blk"], w["b_im_blk"])
        last = l == depth - 1

        mem_k, mem_v = _mem_proj(mem_flat, w["ln_mem_g"], w["w_xk"], w["w_xv"], 512)
        mem_k = mem_k.reshape(bp, n_mem, X_HEADS, X_DH)
        mem_v = mem_v.reshape(bp, n_mem, X_HEADS, X_DH)
        xp, st = _layer(
            xp, w, ssm, mem_k, mem_v, None, None,
            jnp.zeros((bp, ML_HEADS, ML_DH, ML_DH), F32), jnp.zeros((bp, ML_HEADS, ML_DH), F32),
            jnp.zeros((bp, ML_HEADS), F32),
            jnp.zeros((bp, SSM_GROUPS, SSM_P), F32), jnp.zeros((bp, SSM_GROUPS, SSM_P), F32),
            jnp.zeros((bp, CONV_W - 1, D_FF), F32), ln_f, cfg_p, last)
        p_states.append(st + (mem_k, mem_v))

        xs, st = _layer(
            xs, w, ssm, cache_mem_k[l], cache_mem_v[l], cache_sb_k[l], cache_sb_v[l],
            state_mlstm_c[l], state_mlstm_n[l], state_mlstm_m[l],
            state_ssm_re[l], state_ssm_im[l], state_ffn_conv[l], ln_f, cfg_s, last)
        s_states.append(st)

    p_out = [jnp.stack(a) for a in zip(*p_states)]
    s_out = [jnp.stack(a) for a in zip(*s_states)]
    return (xp.reshape(bp, t_p, D_MODEL), xs.reshape(bs, t_s, D_MODEL), *p_out, *s_out)
```

```python
import functools
import math

import jax
import jax.numpy as jnp
from jax import lax
from jax.experimental import pallas as pl
from jax.experimental.pallas import tpu as pltpu

F32 = jnp.float32
BF16 = jnp.bfloat16

D_MODEL = 1024
EPS = 1e-6
CHUNK = 64
ML_HEADS = 4
ML_DH = 96
ML_DHP = 128
ML_W = ML_HEADS * ML_DH
ML_WP = ML_HEADS * ML_DHP
SB_DH = 64
SB_HEADS = 6
SB_W = SB_HEADS * SB_DH
SB_PAIRS = SB_HEADS // 2
SSM_W = 256
SSM_CH = 16
SSM_GROUPS = 16
SSM_P = 64
SSM_N = SSM_GROUPS * SSM_P
X_HEADS = 4
X_DH = 256
N_MEM = 256
D_FF = 2816
CONV_W = 3
FF_CHUNK = 256
KEY_BLOCK = 256
LANE = 128
VMEM_LIMIT = 56 * 1024 * 1024

_OFF_QB = 4 * ML_WP
_OFF_KB = _OFF_QB + SB_W
_OFF_VB = _OFF_KB + SB_W
_OFF_UC = _OFF_VB + SB_W
_N_MAIN = _OFF_UC + SSM_W


def _cparams(*sem):
    return pltpu.CompilerParams(dimension_semantics=sem, vmem_limit_bytes=VMEM_LIMIT)


def _const_spec(shape):
    nd = len(shape)
    return pl.BlockSpec(shape, lambda *_: (0,) * nd, pipeline_mode=pl.Buffered(1))


def _rmsnorm(x, g):
    ms = jnp.mean(x * x, axis=-1, keepdims=True)
    return x * lax.rsqrt(ms + EPS) * g


def _sigmoid(x):
    return 1.0 / (1.0 + jnp.exp(-x))


def _log_sigmoid(x):
    return jnp.minimum(x, 0.0) - jnp.log(1.0 + jnp.exp(-jnp.abs(x)))


def _dot(a, b):
    return jnp.dot(a, b, preferred_element_type=F32)


def _dot_nt(a, b):
    return lax.dot_general(a, b, (((1,), (1,)), ((), ())), preferred_element_type=F32)


def _dot_tn(a, b):
    return lax.dot_general(a, b, (((0,), (0,)), ((), ())), preferred_element_type=F32)


def _split_dot(x, m_bf16, terms):
    acc = None
    r = x
    for i in range(terms):
        p = r.astype(BF16)
        d = _dot(p, m_bf16)
        acc = d if acc is None else acc + d
        if i + 1 < terms:
            r = r - p.astype(F32)
    return acc


def _in_proj_body(x_ref, g_ref, w_ref, b_ref, wg_ref, bg_ref,
                  za_ref, qb_ref, kb_ref, vb_ref, uc_ref, gt_ref):
    h = _rmsnorm(x_ref[...], g_ref[...]).astype(BF16)

    def proj(lo, hi):
        return _dot(h, w_ref[:, lo:hi]) + b_ref[:, lo:hi]

    za_ref[...] = proj(0, _OFF_QB)
    qb_ref[...] = proj(_OFF_QB, _OFF_KB)
    kb_ref[...] = proj(_OFF_KB, _OFF_VB)
    vb_ref[...] = proj(_OFF_VB, _OFF_UC)
    uc_ref[...] = proj(_OFF_UC, _N_MAIN)
    gt_ref[0] = _dot_nt(wg_ref[...], h) + bg_ref[...]


def _in_proj(x, g, w, b, wg, bg, bsz, t_len, tm):
    n = bsz * t_len
    nt = t_len // tm
    row = lambda width: pl.BlockSpec((tm, width), lambda bi, ti: (bi * nt + ti, 0))
    return pl.pallas_call(
        _in_proj_body,
        grid=(bsz, nt),
        in_specs=[row(D_MODEL), _const_spec((1, D_MODEL)), _const_spec((D_MODEL, _N_MAIN)),
                  _const_spec((1, _N_MAIN)), _const_spec((8, D_MODEL)), _const_spec((8, 1))],
        out_specs=[row(4 * ML_WP), row(SB_W), row(SB_W), row(SB_W),
                   pl.BlockSpec((tm, SSM_W), lambda bi, ti: (ti, bi)),
                   pl.BlockSpec((1, 8, tm), lambda bi, ti: (bi * nt + ti, 0, 0))],
        out_shape=[jax.ShapeDtypeStruct((n, 4 * ML_WP), F32),
                   jax.ShapeDtypeStruct((n, SB_W), F32),
                   jax.ShapeDtypeStruct((n, SB_W), F32),
                   jax.ShapeDtypeStruct((n, SB_W), F32),
                   jax.ShapeDtypeStruct((t_len, bsz * SSM_W), F32),
                   jax.ShapeDtypeStruct((bsz * nt, 8, tm), F32)],
        compiler_params=_cparams("parallel", "parallel"),
        name="in_proj",
    )(x, g, w, b, wg, bg)


def _mlstm_body(blk, q_ref, k_ref, v_ref, o_ref, g_ref, c0_ref, n0_ref, m0_ref, gn_ref,
                ya_ref, c_ref, n_ref, m_ref):
    @pl.when(pl.program_id(1) == 0)
    def _():
        c_ref[...] = c0_ref[...]
        n_ref[...] = n0_ref[...]
        m_ref[...] = m0_ref[...]

    nb, ts = q_ref.shape[0], q_ref.shape[1]
    t_i = lax.broadcasted_iota(jnp.int32, (blk, blk), 0)
    s_i = lax.broadcasted_iota(jnp.int32, (blk, blk), 1)
    causal = s_i <= t_i
    eye = s_i == t_i
    upper = jnp.where(t_i <= s_i, 1.0, 0.0).astype(BF16)
    k_scale = ML_DH ** -0.5

    units = [(bi, h) for bi in range(nb) for h in range(ML_HEADS)]
    each = lambda f: [f(u) for u in range(len(units))]
    for ci in range(ts // blk):
        rows = slice(ci * blk, (ci + 1) * blk)
        cols = lambda u: slice(units[u][1] * ML_DHP, (units[u][1] + 1) * ML_DHP)
        g = [g_ref[bi, 0, :, rows] for bi in range(nb)]
        lf_all = [_log_sigmoid(x) for x in g]
        b_rows = [_split_dot(x, upper, 3) for x in lf_all]
        q = each(lambda u: q_ref[units[u][0], rows, cols(u)])
        k = each(lambda u: k_ref[units[u][0], rows, cols(u)] * k_scale)
        v = each(lambda u: v_ref[units[u][0], rows, cols(u)])
        qb = each(lambda u: q[u].astype(BF16))
        kb = each(lambda u: k[u].astype(BF16))
        vb = each(lambda u: v[u].astype(BF16))
        lf = each(lambda u: lf_all[units[u][0]][4 + units[u][1]:5 + units[u][1], :])
        a_row = each(lambda u: g[units[u][0]][units[u][1]:units[u][1] + 1, :]
                     - b_rows[units[u][0]][4 + units[u][1]:5 + units[u][1], :])
        b_col = each(lambda u: jnp.sum(jnp.where(causal, lf[u], 0.0), axis=-1, keepdims=True))
        m_prev = each(lambda u: m_ref[units[u][0], units[u][1]][:, 0:1])
        n_row = each(lambda u: n_ref[units[u][0], units[u][1]])
        ct = each(lambda u: c_ref[units[u][0], units[u][1]])

        a_m = each(lambda u: jnp.where(causal, a_row[u], -jnp.inf))
        m_col = each(lambda u: jnp.maximum(jnp.max(a_m[u], axis=-1, keepdims=True), m_prev[u]))
        w_intra = each(lambda u: jnp.exp(a_m[u] - m_col[u]))
        w_inter = each(lambda u: jnp.exp(m_prev[u] - m_col[u]))
        qk = each(lambda u: _dot_nt(qb[u], kb[u]))
        qc = each(lambda u: _dot(qb[u], ct[u].astype(BF16)))
        s = each(lambda u: qk[u] * w_intra[u])
        sv = each(lambda u: _dot(s[u].astype(BF16), vb[u]))
        qn = each(lambda u: jnp.sum(q[u] * n_row[u], axis=-1, keepdims=True))
        den = each(lambda u: jnp.sum(s[u], axis=-1, keepdims=True) + w_inter[u] * qn[u])
        num = each(lambda u: sv[u] + w_inter[u] * qc[u])
        hh = each(lambda u: num[u] / jnp.maximum(jnp.abs(den[u]), jnp.exp(-(b_col[u] + m_col[u]))))
        x = each(lambda u: _sigmoid(o_ref[units[u][0], rows, cols(u)]) * hh[u])
        ms = each(lambda u: jnp.sum(x[u] * x[u], axis=-1, keepdims=True) * (1.0 / ML_DH))
        for u, (bi, h) in enumerate(units):
            ya_ref[bi, rows, cols(u)] = x[u] * lax.rsqrt(ms[u] + EPS) * gn_ref[:, cols(u)]

        m_last = each(lambda u: m_col[u][blk - 1:blk, :])
        wg_row = each(lambda u: jnp.exp(a_row[u] - m_last[u]))
        wg_col = each(lambda u: jnp.sum(jnp.where(eye, wg_row[u], 0.0), axis=-1, keepdims=True))
        decay = each(lambda u: jnp.exp(m_prev[u] - m_last[u]))
        kv = each(lambda u: _dot_tn((k[u] * wg_col[u]).astype(BF16), vb[u]))
        kn = each(lambda u: _dot(jnp.broadcast_to(wg_row[u], (8, blk)).astype(BF16), kb[u])[0:1, :])
        for u, (bi, h) in enumerate(units):
            c_ref[bi, h] = decay[u] * ct[u] + kv[u]
            n_ref[bi, h] = decay[u] * n_row[u] + kn[u]
            m_ref[bi, h] = jnp.broadcast_to(b_col[u][blk - 1:blk, :] + m_last[u], (1, LANE))


def _mlstm(za, gates, c0, n0, m0, gn, bsz, t_len, ts, blk, nb):
    ns = t_len // ts
    col = lambda j: pl.BlockSpec((nb, ts, ML_WP), lambda bi, si: (bi, si, j))
    st4 = pl.BlockSpec((nb, ML_HEADS, ML_DHP, ML_DHP), lambda bi, si: (bi, 0, 0, 0))
    st3 = pl.BlockSpec((nb, ML_HEADS, 1, LANE), lambda bi, si: (bi, 0, 0, 0))
    return pl.pallas_call(
        functools.partial(_mlstm_body, blk),
        grid=(bsz // nb, ns),
        in_specs=[col(0), col(1), col(2), col(3),
                  pl.BlockSpec((nb, 1, 8, ts), lambda bi, si: (bi, si, 0, 0)),
                  st4, st3, st3, _const_spec((1, ML_WP))],
        out_specs=[pl.BlockSpec((nb, ts, ML_WP), lambda bi, si: (bi, si, 0)), st4, st3, st3],
        out_shape=[jax.ShapeDtypeStruct((bsz, t_len, ML_WP), F32),
                   jax.ShapeDtypeStruct((bsz, ML_HEADS, ML_DHP, ML_DHP), F32),
                   jax.ShapeDtypeStruct((bsz, ML_HEADS, 1, LANE), F32),
                   jax.ShapeDtypeStruct((bsz, ML_HEADS, 1, LANE), F32)],
        compiler_params=_cparams("parallel", "arbitrary"),
        name="mlstm",
    )(za, za, za, za, gates, c0, n0, m0, gn)


def _sb_body(past_off, n_q, q_ref, kc_ref, vc_ref, kp_ref, vp_ref, gn_ref, y_ref, acc_ref, car_ref):
    tq = q_ref.shape[1]
    qi = pl.program_id(1)
    lane = lax.broadcasted_iota(jnp.int32, (1, LANE), 1)
    first = lane < SB_DH
    q = q_ref[0] * (SB_DH ** -0.5)
    qh = []
    for p in range(SB_PAIRS):
        qp = q[:, p * LANE:(p + 1) * LANE]
        qh += [jnp.where(first, qp, 0.0).astype(BF16), jnp.where(first, 0.0, qp).astype(BF16)]

    def ones_and_after(kb):
        r = lax.broadcasted_iota(jnp.int32, (kb, kb), 0)
        c = lax.broadcasted_iota(jnp.int32, (kb, kb), 1)
        return jnp.where(r >= c, 1.0, 0.0).astype(BF16)

    def block(k, v, uo, vis):
        kb = k.shape[0]
        heads = range(SB_HEADS)
        kbf = [k[:, p * LANE:(p + 1) * LANE].astype(BF16) for p in range(SB_PAIRS)]
        vbf = [v[:, p * LANE:(p + 1) * LANE].astype(BF16) for p in range(SB_PAIRS)]
        z = [_dot_nt(qh[hd], kbf[hd // 2]) for hd in heads]
        l1m = [-(jnp.maximum(z[hd], 0.0) + jnp.log(1.0 + jnp.exp(-jnp.abs(z[hd])))) for hd in heads]
        lmask = l1m if vis is None else [jnp.where(vis, l1m[hd], 0.0) for hd in heads]
        cs = [_split_dot(lmask[hd], uo, 2) for hd in heads]
        rs = [jnp.broadcast_to(cs[hd][:, 0:1], (tq, LANE)) for hd in heads]
        if vis is None:
            car = [car_ref[hd] for hd in heads]
            car = [c[:, :kb] if kb <= LANE else jnp.concatenate([c] * (kb // LANE), axis=1) for c in car]
            a = [jnp.exp(z[hd] + (car[hd] + cs[hd])) for hd in heads]
            pv = [_dot(a[hd].astype(BF16), vbf[hd // 2]) for hd in heads]
            for hd in heads:
                acc_ref[hd] += pv[hd]
                car_ref[hd] += rs[hd]
        else:
            a = [jnp.where(vis, jnp.exp(z[hd] + cs[hd]), 0.0) for hd in heads]
            pv = [_dot(a[hd].astype(BF16), vbf[hd // 2]) for hd in heads]
            for hd in heads:
                acc_ref[hd] = pv[hd]
                car_ref[hd] = rs[hd]

    uo_cur = ones_and_after(tq)
    uo_past = uo_cur if tq == KEY_BLOCK else ones_and_after(KEY_BLOCK)
    t_i = lax.broadcasted_iota(jnp.int32, (tq, tq), 0)
    s_i = lax.broadcasted_iota(jnp.int32, (tq, tq), 1)
    block(kc_ref[0], vc_ref[0], uo_cur, s_i < t_i)

    if n_q == 1:
        for j in reversed(range(past_off // KEY_BLOCK)):
            block(kp_ref[0, j * KEY_BLOCK:(j + 1) * KEY_BLOCK, :], vp_ref[0, j * KEY_BLOCK:(j + 1) * KEY_BLOCK, :],
                  uo_past, None)
    else:
        n_past = (past_off + qi * tq) // KEY_BLOCK

        def past(j, carry):
            start = pl.multiple_of((n_past - 1 - j) * KEY_BLOCK, KEY_BLOCK)
            block(kp_ref[0, pl.ds(start, KEY_BLOCK), :], vp_ref[0, pl.ds(start, KEY_BLOCK), :], uo_past, None)
            return carry

        lax.fori_loop(0, n_past, past, 0)

    for p in range(SB_PAIRS):
        o = jnp.where(first, acc_ref[2 * p], acc_ref[2 * p + 1])
        o2 = o * o
        ss0 = jnp.sum(jnp.where(first, o2, 0.0), axis=-1, keepdims=True)
        ss1 = jnp.sum(jnp.where(first, 0.0, o2), axis=-1, keepdims=True)
        inv = jnp.where(first, lax.rsqrt(ss0 * (1.0 / SB_DH) + EPS), lax.rsqrt(ss1 * (1.0 / SB_DH) + EPS))
        y_ref[0, :, p * LANE:(p + 1) * LANE] = o * inv * gn_ref[:, p * LANE:(p + 1) * LANE]


def _sb_attn(q, kc, vc, kp, vp, gn, tq, past_off):
    bsz, t_len, _ = q.shape
    p_len = kp.shape[1]
    n_q = t_len // tq
    assert (past_off % KEY_BLOCK == 0) and (n_q == 1 or tq % KEY_BLOCK == 0)
    cur = pl.BlockSpec((1, tq, SB_W), lambda bi, qi: (bi, qi, 0))
    past = pl.BlockSpec((1, p_len, SB_W), lambda bi, qi: (bi, 0, 0))
    return pl.pallas_call(
        functools.partial(_sb_body, past_off, n_q),
        grid=(bsz, n_q),
        in_specs=[cur, cur, cur, past, past, _const_spec((1, SB_W))],
        out_specs=cur,
        out_shape=jax.ShapeDtypeStruct((bsz, t_len, SB_W), F32),
        scratch_shapes=[pltpu.VMEM((SB_HEADS, tq, LANE), F32), pltpu.VMEM((SB_HEADS, tq, LANE), F32)],
        compiler_params=_cparams("parallel", "arbitrary"),
        name="sb_attn",
    )(q, kc, vc, kp, vp, gn)


def _s5_prep_body(are_ref, aim_ref, ldt_ref, bre_ref, bim_ref, ab_ref, bcat_ref):
    a_re, a_im = are_ref[...], aim_ref[...]
    dt = jnp.exp(ldt_ref[...])
    mag = jnp.exp(a_re * dt)
    ab_re = mag * jnp.cos(a_im * dt)
    ab_im = mag * jnp.sin(a_im * dt)
    den = a_re * a_re + a_im * a_im
    nr = ab_re - 1.0
    zr = (nr * a_re + ab_im * a_im) / den
    zi = (ab_im * a_re - nr * a_im) / den
    ab_ref[0:1, :] = ab_re
    ab_ref[1:2, :] = ab_im
    b_re, b_im = bre_ref[...], bim_ref[...]
    bcat_ref[:, :SSM_N] = (zr * b_re - zi * b_im).astype(BF16)
    bcat_ref[:, SSM_N:] = (zr * b_im + zi * b_re).astype(BF16)


def _s5_prep(a_re, a_im, log_dt, b_re_blk, b_im_blk):
    return pl.pallas_call(
        _s5_prep_body,
        out_shape=[jax.ShapeDtypeStruct((2, SSM_N), F32), jax.ShapeDtypeStruct((SSM_W, 2 * SSM_N), BF16)],
        name="s5_prep",
    )(a_re, a_im, log_dt, b_re_blk, b_im_blk)


def _gelu_tanh(x):
    return 0.5 * x * (1.0 + jnp.tanh(math.sqrt(2.0 / math.pi) * (x + 0.044715 * (x * x * x))))


def _s5_body(u_ref, bcat_ref, ab_ref, cre_ref, cim_ref, d_ref, wglu_ref, bglu_ref, gn_ref,
             h0r_ref, h0i_ref, y_ref, hr_ref, hi_ref, xs_ref):
    tt, bsz, _ = u_ref.shape

    @pl.when(pl.program_id(0) == 0)
    def _():
        hr_ref[...] = h0r_ref[...]
        hi_ref[...] = h0i_ref[...]

    u2 = u_ref[...].reshape(tt * bsz, SSM_W)
    xs_ref[...] = _dot(u2.astype(BF16), bcat_ref[...])
    ar = jnp.broadcast_to(ab_ref[0:1, :], (bsz, SSM_N))
    ai = jnp.broadcast_to(ab_ref[1:2, :], (bsz, SSM_N))

    def step(t, carry):
        hr, hi = carry
        off = pl.multiple_of(t * bsz, bsz)
        xr = xs_ref[pl.ds(off, bsz), :SSM_N]
        xi = xs_ref[pl.ds(off, bsz), SSM_N:]
        nr = ar * hr - ai * hi + xr
        ni = ar * hi + ai * hr + xi
        xs_ref[pl.ds(off, bsz), :SSM_N] = nr
        xs_ref[pl.ds(off, bsz), SSM_N:] = ni
        return nr, ni

    hr, hi = lax.fori_loop(0, tt, step, (hr_ref[...], hi_ref[...]))
    hr_ref[...] = hr
    hi_ref[...] = hi

    y = (_dot(xs_ref[:, :SSM_N].astype(BF16), cre_ref[...])
         - _dot(xs_ref[:, SSM_N:].astype(BF16), cim_ref[...])
         + d_ref[...] * u2)
    g = _gelu_tanh(y)
    gate = _sigmoid(_dot(g.astype(BF16), wglu_ref[...]) + bglu_ref[...])
    y_ref[...] = _rmsnorm(g * gate, gn_ref[...]).reshape(tt, bsz, SSM_W)


def _s5(u_tm, bcat, ab, c_re, c_im, d, w_glu, b_glu, gn, h0r, h0i, tt):
    t_len, bsz, _ = u_tm.shape
    tile = pl.BlockSpec((tt, bsz, SSM_W), lambda ti: (ti, 0, 0))
    st = pl.BlockSpec((bsz, SSM_N), lambda ti: (0, 0))
    return pl.pallas_call(
        _s5_body,
        grid=(t_len // tt,),
        in_specs=[tile, _const_spec((SSM_W, 2 * SSM_N)), _const_spec((2, SSM_N)),
                  _const_spec((SSM_N, SSM_W)), _const_spec((SSM_N, SSM_W)), _const_spec((1, SSM_W)),
                  _const_spec((SSM_W, SSM_W)), _const_spec((1, SSM_W)), _const_spec((1, SSM_W)), st, st],
        out_specs=[tile, st, st],
        out_shape=[jax.ShapeDtypeStruct((t_len, bsz, SSM_W), F32),
                   jax.ShapeDtypeStruct((bsz, SSM_N), F32),
                   jax.ShapeDtypeStruct((bsz, SSM_N), F32)],
        scratch_shapes=[pltpu.VMEM((tt * bsz, 2 * SSM_N), F32)],
        compiler_params=_cparams("arbitrary"),
        name="s5",
    )(u_tm, bcat, ab, c_re, c_im, d, w_glu, b_glu, gn, h0r, h0i)


def _out_proj_body(x_ref, ya_ref, yb_ref, yc_ref, woa_ref, wob_ref, woc_ref, lnx_ref, wxq_ref,
                   x1_ref, qx_ref):
    x1 = (x_ref[...]
          + _dot(ya_ref[...].astype(BF16), woa_ref[...])
          + _dot(yb_ref[...].astype(BF16), wob_ref[...])
          + _dot(yc_ref[...].astype(BF16), woc_ref[...]))
    x1_ref[...] = x1
    hq = _rmsnorm(x1, lnx_ref[...]).astype(BF16)
    qx_ref[...] = _dot(hq, wxq_ref[...]).astype(BF16)


def _out_proj(x, ya, yb, yc_tm, woa, wob, woc, lnx, wxq, bsz, t_len, tm):
    n = bsz * t_len
    nt = t_len // tm
    row = lambda width: pl.BlockSpec((tm, width), lambda bi, ti: (bi * nt + ti, 0))
    return pl.pallas_call(
        _out_proj_body,
        grid=(bsz, nt),
        in_specs=[row(D_MODEL), row(ML_WP), row(SB_W),
                  pl.BlockSpec((tm, SSM_W), lambda bi, ti: (ti, bi)),
                  _const_spec((ML_WP, D_MODEL)), _const_spec((SB_W, D_MODEL)), _const_spec((SSM_W, D_MODEL)),
                  _const_spec((1, D_MODEL)), _const_spec((D_MODEL, D_MODEL))],
        out_specs=[row(D_MODEL), row(D_MODEL)],
        out_shape=[jax.ShapeDtypeStruct((n, D_MODEL), F32), jax.ShapeDtypeStruct((n, D_MODEL), BF16)],
        compiler_params=_cparams("parallel", "parallel"),
        name="out_proj",
    )(x, ya, yb, yc_tm, woa, wob, woc, lnx, wxq)


def _mem_proj_body(m_ref, g_ref, wk_ref, wv_ref, k_ref, v_ref):
    mn = _rmsnorm(m_ref[...], g_ref[...]).astype(BF16)
    k_ref[...] = _dot(mn, wk_ref[...])
    v_ref[...] = _dot(mn, wv_ref[...])


def _mem_proj(mem, g, wk, wv, tm):
    n = mem.shape[0]
    row = pl.BlockSpec((tm, D_MODEL), lambda i: (i, 0))
    return pl.pallas_call(
        _mem_proj_body,
        grid=(n // tm,),
        in_specs=[row, _const_spec((1, D_MODEL)), _const_spec((D_MODEL, D_MODEL)), _const_spec((D_MODEL, D_MODEL))],
        out_specs=[row, row],
        out_shape=[jax.ShapeDtypeStruct((n, D_MODEL), F32)] * 2,
        compiler_params=_cparams("parallel"),
        name="mem_proj",
    )(mem, g, wk, wv)


def _x_attn_body(x_ref, q_ref, k_ref, v_ref, wo_ref, o_ref):
    acc = x_ref[...]
    for h in range(X_HEADS):
        c0, c1 = h * X_DH, (h + 1) * X_DH
        kh = k_ref[0, :, c0:c1].astype(BF16)
        vh = v_ref[0, :, c0:c1].astype(BF16)
        s = _dot_nt(q_ref[:, c0:c1], kh) * (X_DH ** -0.5)
        e = jnp.exp(s - jnp.max(s, axis=-1, keepdims=True))
        p = e / jnp.sum(e, axis=-1, keepdims=True)
        oh = _dot(p.astype(BF16), vh)
        acc = acc + _dot(oh.astype(BF16), wo_ref[c0:c1, :])
    o_ref[...] = acc


def _x_attn(x1, qx, mem_k, mem_v, wo, bsz, t_len, tq):
    n = bsz * t_len
    nt = t_len // tq
    row = pl.BlockSpec((tq, D_MODEL), lambda bi, ti: (bi * nt + ti, 0))
    mem = pl.BlockSpec((1, N_MEM, D_MODEL), lambda bi, ti: (bi, 0, 0))
    return pl.pallas_call(
        _x_attn_body,
        grid=(bsz, nt),
        in_specs=[row, row, mem, mem, _const_spec((D_MODEL, D_MODEL))],
        out_specs=row,
        out_shape=jax.ShapeDtypeStruct((n, D_MODEL), F32),
        compiler_params=_cparams("parallel", "parallel"),
        name="x_attn",
    )(x1, qx, mem_k, mem_v, wo)


def _conv_ffn_body(final_norm, x_ref, g_ref, wa_ref, wb_ref, cw_ref, cb_ref, wd_ref, prev_ref, gf_ref,
                   o_ref, conv_ref, abuf_ref, acc_ref):
    tm = x_ref.shape[0]

    @pl.when(pl.program_id(1) == 0)
    def _():
        conv_ref[...] = prev_ref[...]

    x = x_ref[...]
    hb = _rmsnorm(x, g_ref[...]).astype(BF16)
    acc_ref[...] = x
    for c in range(D_FF // FF_CHUNK):
        c0, c1 = c * FF_CHUNK, (c + 1) * FF_CHUNK
        a = _dot(hb, wa_ref[:, c0:c1])
        abuf_ref[6:8, :] = conv_ref[0, :, c0:c1]
        abuf_ref[8:8 + tm, :] = a
        conv_ref[0, :, c0:c1] = a[tm - 2:tm, :]
        cv = (cb_ref[:, c0:c1]
              + abuf_ref[6:6 + tm, :] * cw_ref[0:1, c0:c1]
              + abuf_ref[7:7 + tm, :] * cw_ref[1:2, c0:c1]
              + a * cw_ref[2:3, c0:c1])
        y = cv * _sigmoid(cv) * _dot(hb, wb_ref[:, c0:c1])
        acc_ref[...] += _dot(y.astype(BF16), wd_ref[c0:c1, :])
    out = acc_ref[...]
    if final_norm:
        out = _rmsnorm(out, gf_ref[...])
    o_ref[...] = out


def _conv_ffn(x, g, wa, wb, cw, cb, wd, prev, gf, bsz, t_len, tm, final_norm):
    n = bsz * t_len
    nt = t_len // tm
    row = pl.BlockSpec((tm, D_MODEL), lambda bi, ti: (bi * nt + ti, 0))
    conv = pl.BlockSpec((1, CONV_W - 1, D_FF), lambda bi, ti: (bi, 0, 0))
    return pl.pallas_call(
        functools.partial(_conv_ffn_body, final_norm),
        grid=(bsz, nt),
        in_specs=[row, _const_spec((1, D_MODEL)), _const_spec((D_MODEL, D_FF)), _const_spec((D_MODEL, D_FF)),
                  _const_spec((CONV_W, D_FF)), _const_spec((1, D_FF)), _const_spec((D_FF, D_MODEL)),
                  conv, _const_spec((1, D_MODEL))],
        out_specs=[row, conv],
        out_shape=[jax.ShapeDtypeStruct((n, D_MODEL), F32),
                   jax.ShapeDtypeStruct((bsz, CONV_W - 1, D_FF), F32)],
        scratch_shapes=[pltpu.VMEM((tm + 8, FF_CHUNK), F32), pltpu.VMEM((tm, D_MODEL), F32)],
        compiler_params=_cparams("parallel", "arbitrary"),
        name="conv_ffn",
    )(x, g, wa, wb, cw, cb, wd, prev, gf)


def _pad_heads(a, axis):
    shp = a.shape
    a = a.reshape(shp[:axis] + (ML_HEADS, ML_DH) + shp[axis + 1:])
    pad = [(0, 0)] * a.ndim
    pad[axis + 1] = (0, ML_DHP - ML_DH)
    a = jnp.pad(a, pad)
    return a.reshape(shp[:axis] + (ML_WP,) + shp[axis + 1:])


def _block_diag(blocks):
    g, r, c = blocks.shape
    eye = jnp.eye(g, dtype=blocks.dtype)
    return (eye[:, None, :, None] * blocks[:, :, None, :]).reshape(g * r, g * c)


def _pack_layer(p):
    w_in, b_in = p["w_in"], p["b_in"]
    o = 0
    parts_w, parts_b = [], []
    for _ in range(4):
        parts_w.append(_pad_heads(w_in[:, o:o + ML_W], 1))
        parts_b.append(_pad_heads(b_in[o:o + ML_W], 0))
        o += ML_W
    wg = w_in[:, o:o + 2 * ML_HEADS].T
    bg = b_in[o:o + 2 * ML_HEADS][:, None]
    o += 2 * ML_HEADS
    parts_w.append(w_in[:, o:])
    parts_b.append(b_in[o:])
    w_out = p["w_out"]
    return dict(
        ln_mix_g=p["ln_mix_g"][None], w_main=jnp.concatenate(parts_w, axis=1).astype(BF16),
        b_main=jnp.concatenate(parts_b)[None], wg=wg.astype(BF16), bg=bg,
        gn_a=_pad_heads(p["gn_a_g"], 0)[None], gn_b=p["gn_b_g"][None], gn_c=p["gn_c_g"][None],
        a_re=p["ssm_a_re"].reshape(1, SSM_N), a_im=p["ssm_a_im"].reshape(1, SSM_N),
        log_dt=p["ssm_log_dt"].reshape(1, SSM_N),
        b_re_blk=_block_diag(jnp.swapaxes(p["ssm_b_re"], 1, 2)),
        b_im_blk=_block_diag(jnp.swapaxes(p["ssm_b_im"], 1, 2)),
        c_re_blk=_block_diag(jnp.swapaxes(p["ssm_c_re"], 1, 2)).astype(BF16),
        c_im_blk=_block_diag(jnp.swapaxes(p["ssm_c_im"], 1, 2)).astype(BF16),
        ssm_d=p["ssm_d"][None], w_glu=p["w_glu"].astype(BF16), b_glu=p["b_glu"][None],
        w_out_a=_pad_heads(w_out[:ML_W], 0).astype(BF16), w_out_b=w_out[ML_W:ML_W + SB_W].astype(BF16),
        w_out_c=w_out[ML_W + SB_W:].astype(BF16),
        ln_x_g=p["ln_x_g"][None], w_xq=p["w_xq"].astype(BF16), w_xo=p["w_xo"].astype(BF16),
        ln_mem_g=p["ln_mem_g"][None], w_xk=p["w_xk"].astype(BF16), w_xv=p["w_xv"].astype(BF16),
        ln_ffn_g=p["ln_ffn_g"][None], w_ffn_a=p["w_ffn_a"].astype(BF16), w_ffn_b=p["w_ffn_b"].astype(BF16),
        ffn_conv_w=p["ffn_conv_w"], ffn_conv_b=p["ffn_conv_b"][None], w_ffn_down=p["w_ffn_down"].astype(BF16),
    )


def _layer(x, w, ssm, mem_k, mem_v, past_k, past_v, c0, n0, m0, s0_re, s0_im, conv_prev, ln_f, cfg, final_norm):
    bsz, t_len = cfg["bsz"], cfg["t_len"]
    tm, ts, blk = cfg["tm"], cfg["ts"], cfg["blk"]
    ab, bcat = ssm

    za, qb, kb, vb, u_tm, gates = _in_proj(x, w["ln_mix_g"], w["w_main"], w["b_main"], w["wg"], w["bg"], bsz, t_len, tm)
    if tm != ts:
        gates = gates.reshape(-1, 8, tm // ts, ts).transpose(0, 2, 1, 3).reshape(-1, 8, ts)

    ct0 = jnp.pad(jnp.swapaxes(c0, -1, -2), ((0, 0), (0, 0), (0, ML_DHP - ML_DH), (0, ML_DHP - ML_DH)))
    n0p = jnp.pad(n0, ((0, 0), (0, 0), (0, ML_DHP - ML_DH)))[:, :, None, :]
    m0p = jnp.broadcast_to(m0[:, :, None, None], (bsz, ML_HEADS, 1, LANE))
    ya, ct, n_new, m_new = _mlstm(za.reshape(bsz, t_len, 4 * ML_WP), gates.reshape(bsz, t_len // ts, 8, ts),
                                  ct0, n0p, m0p, w["gn_a"], bsz, t_len, ts, blk, cfg["nb_ml"])
    ya = ya.reshape(bsz * t_len, ML_WP)

    q3, k3, v3 = (a.reshape(bsz, t_len, SB_W) for a in (qb, kb, vb))
    if past_k is None:
        yb = _sb_attn(q3, k3, v3, k3, v3, w["gn_b"], cfg["tq_sb"], 0)
    else:
        p_len = past_k.shape[1]
        yb = _sb_attn(q3, k3, v3, past_k.reshape(bsz, p_len, SB_W), past_v.reshape(bsz, p_len, SB_W),
                      w["gn_b"], cfg["tq_sb"], p_len)
    yb = yb.reshape(bsz * t_len, SB_W)

    yc_tm, s_re, s_im = _s5(u_tm.reshape(t_len, bsz, SSM_W), bcat, ab, w["c_re_blk"], w["c_im_blk"], w["ssm_d"],
                            w["w_glu"], w["b_glu"], w["gn_c"], s0_re.reshape(bsz, SSM_N), s0_im.reshape(bsz, SSM_N),
                            cfg["tt"])

    x1, qx = _out_proj(x, ya, yb, yc_tm.reshape(t_len, bsz * SSM_W), w["w_out_a"], w["w_out_b"], w["w_out_c"],
                       w["ln_x_g"], w["w_xq"], bsz, t_len, tm)
    x2 = _x_attn(x1, qx, mem_k.reshape(bsz, N_MEM, D_MODEL), mem_v.reshape(bsz, N_MEM, D_MODEL), w["w_xo"],
                 bsz, t_len, tm)
    x3, conv_new = _conv_ffn(x2, w["ln_ffn_g"], w["w_ffn_a"], w["w_ffn_b"], w["ffn_conv_w"], w["ffn_conv_b"],
                             w["w_ffn_down"], conv_prev, ln_f, bsz, t_len, cfg["tm_ffn"], final_norm)

    states = (kb.reshape(bsz, t_len, SB_HEADS, SB_DH), vb.reshape(bsz, t_len, SB_HEADS, SB_DH),
              jnp.swapaxes(ct[:, :, :ML_DH, :ML_DH], -1, -2), n_new[:, :, 0, :ML_DH], m_new[:, :, 0, 0],
              s_re.reshape(bsz, SSM_GROUPS, SSM_P), s_im.reshape(bsz, SSM_GROUPS, SSM_P), conv_new)
    return x3, states


_PROMPT_CFG = dict(tm=512, ts=128, blk=CHUNK, nb_ml=4, tq_sb=256, tt=64, tm_ffn=512)
_SAMPLE_CFG = dict(tm=32, ts=32, blk=32, nb_ml=4, tq_sb=32, tt=32, tm_ffn=32)


def kernel(x_prompt, x_sample, cache_sb_k, cache_sb_v, state_mlstm_c, state_mlstm_n, state_mlstm_m, state_ssm_re, state_ssm_im, state_ffn_conv, cache_mem_k, cache_mem_v, mem_prompt, ln_mix_g, w_in, b_in, gn_a_g, gn_b_g, gn_c_g, ssm_a_re, ssm_a_im, ssm_log_dt, ssm_b_re, ssm_b_im, ssm_c_re, ssm_c_im, ssm_d, w_glu, b_glu, w_out, ln_x_g, ln_mem_g, w_xq, w_xk, w_xv, w_xo, ln_ffn_g, w_ffn_a, w_ffn_b, ffn_conv_w, ffn_conv_b, w_ffn_down, ln_f_g):
    per_layer = dict(ln_mix_g=ln_mix_g, w_in=w_in, b_in=b_in, gn_a_g=gn_a_g, gn_b_g=gn_b_g, gn_c_g=gn_c_g,
                     ssm_a_re=ssm_a_re, ssm_a_im=ssm_a_im, ssm_log_dt=ssm_log_dt, ssm_b_re=ssm_b_re,
                     ssm_b_im=ssm_b_im, ssm_c_re=ssm_c_re, ssm_c_im=ssm_c_im, ssm_d=ssm_d, w_glu=w_glu,
                     b_glu=b_glu, w_out=w_out, ln_x_g=ln_x_g, ln_mem_g=ln_mem_g, w_xq=w_xq, w_xk=w_xk,
                     w_xv=w_xv, w_xo=w_xo, ln_ffn_g=ln_ffn_g, w_ffn_a=w_ffn_a, w_ffn_b=w_ffn_b,
                     ffn_conv_w=ffn_conv_w, ffn_conv_b=ffn_conv_b, w_ffn_down=w_ffn_down)
    depth = w_in.shape[0]
    bp, t_p, _ = x_prompt.shape
    bs, t_s, _ = x_sample.shape
    n_mem = mem_prompt.shape[1]
    cfg_p = dict(_PROMPT_CFG, bsz=bp, t_len=t_p)
    cfg_s = dict(_SAMPLE_CFG, bsz=bs, t_len=t_s)
    ln_f = ln_f_g[None]

    xp = x_prompt.reshape(bp * t_p, D_MODEL)
    xs = x_sample.reshape(bs * t_s, D_MODEL)
    mem_flat = mem_prompt.reshape(bp * n_mem, D_MODEL)
    p_states, s_states = [], []
    for l in range(depth):
        w = _pack_layer({k: v[l] for k, v in per_layer.items()})
        ssm = _s5_prep(w["a_re"], w["a_im"], w["log_dt"], w["b_re_blk"], w["b_im_blk"])
        last = l == depth - 1

        mem_k, mem_v = _mem_proj(mem_flat, w["ln_mem_g"], w["w_xk"], w["w_xv"], 512)
        mem_k = mem_k.reshape(bp, n_mem, X_HEADS, X_DH)
        mem_v = mem_v.reshape(bp, n_mem, X_HEADS, X_DH)
        xp, st = _layer(
            xp, w, ssm, mem_k, mem_v, None, None,
            jnp.zeros((bp, ML_HEADS, ML_DH, ML_DH), F32), jnp.zeros((bp, ML_HEADS, ML_DH), F32),
            jnp.zeros((bp, ML_HEADS), F32),
            jnp.zeros((bp, SSM_GROUPS, SSM_P), F32), jnp.zeros((bp, SSM_GROUPS, SSM_P), F32),
            jnp.zeros((bp, CONV_W - 1, D_FF), F32), ln_f, cfg_p, last)
        p_states.append(st + (mem_k, mem_v))

        xs, st = _layer(
            xs, w, ssm, cache_mem_k[l], cache_mem_v[l], cache_sb_k[l], cache_sb_v[l],
            state_mlstm_c[l], state_mlstm_n[l], state_mlstm_m[l],
            state_ssm_re[l], state_ssm_im[l], state_ffn_conv[l], ln_f, cfg_s, last)
        s_states.append(st)

    p_out = [jnp.stack(a) for a in zip(*p_states)]
    s_out = [jnp.stack(a) for a in zip(*s_states)]
    return (xp.reshape(bp, t_p, D_MODEL), xs.reshape(bs, t_s, D_MODEL), *p_out, *s_out)
```

```python
import functools
import math

import jax
import jax.numpy as jnp
from jax import lax
from jax.experimental import pallas as pl
from jax.experimental.pallas import tpu as pltpu

F32 = jnp.float32
BF16 = jnp.bfloat16

D_MODEL = 1024
EPS = 1e-6
CHUNK = 64
ML_HEADS = 4
ML_DH = 96
ML_DHP = 128
ML_W = ML_HEADS * ML_DH
ML_WP = ML_HEADS * ML_DHP
SB_DH = 64
SB_HEADS = 6
SB_W = SB_HEADS * SB_DH
SB_PAIRS = SB_HEADS // 2
SSM_W = 256
SSM_CH = 16
SSM_GROUPS = 16
SSM_P = 64
SSM_N = SSM_GROUPS * SSM_P
X_HEADS = 4
X_DH = 256
N_MEM = 256
D_FF = 2816
CONV_W = 3
FF_CHUNK = 256
KEY_BLOCK = 256
LANE = 128
VMEM_LIMIT = 56 * 1024 * 1024

_OFF_QB = 4 * ML_WP
_OFF_KB = _OFF_QB + SB_W
_OFF_VB = _OFF_KB + SB_W
_OFF_UC = _OFF_VB + SB_W
_N_MAIN = _OFF_UC + SSM_W


def _cparams(*sem):
    return pltpu.CompilerParams(dimension_semantics=sem, vmem_limit_bytes=VMEM_LIMIT)


def _const_spec(shape):
    nd = len(shape)
    return pl.BlockSpec(shape, lambda *_: (0,) * nd, pipeline_mode=pl.Buffered(1))


def _rmsnorm(x, g):
    ms = jnp.mean(x * x, axis=-1, keepdims=True)
    return x * lax.rsqrt(ms + EPS) * g


def _sigmoid(x):
    return 1.0 / (1.0 + jnp.exp(-x))


def _log_sigmoid(x):
    return jnp.minimum(x, 0.0) - jnp.log(1.0 + jnp.exp(-jnp.abs(x)))


def _dot(a, b):
    return jnp.dot(a, b, preferred_element_type=F32)


def _dot_nt(a, b):
    return lax.dot_general(a, b, (((1,), (1,)), ((), ())), preferred_element_type=F32)


def _dot_tn(a, b):
    return lax.dot_general(a, b, (((0,), (0,)), ((), ())), preferred_element_type=F32)


def _split_dot(x, m_bf16, terms):
    acc = None
    r = x
    for i in range(terms):
        p = r.astype(BF16)
        d = _dot(p, m_bf16)
        acc = d if acc is None else acc + d
        if i + 1 < terms:
            r = r - p.astype(F32)
    return acc


def _in_proj_body(x_ref, g_ref, w_ref, b_ref, wg_ref, bg_ref,
                  za_ref, qb_ref, kb_ref, vb_ref, uc_ref, gt_ref):
    h = _rmsnorm(x_ref[...], g_ref[...]).astype(BF16)

    def proj(lo, hi):
        return _dot(h, w_ref[:, lo:hi]) + b_ref[:, lo:hi]

    za_ref[...] = proj(0, _OFF_QB)
    qb_ref[...] = proj(_OFF_QB, _OFF_KB)
    kb_ref[...] = proj(_OFF_KB, _OFF_VB)
    vb_ref[...] = proj(_OFF_VB, _OFF_UC)
    uc_ref[...] = proj(_OFF_UC, _N_MAIN)
    gt_ref[0] = _dot_nt(wg_ref[...], h) + bg_ref[...]


def _in_proj(x, g, w, b, wg, bg, bsz, t_len, tm):
    n = bsz * t_len
    nt = t_len // tm
    row = lambda width: pl.BlockSpec((tm, width), lambda bi, ti: (bi * nt + ti, 0))
    return pl.pallas_call(
        _in_proj_body,
        grid=(bsz, nt),
        in_specs=[row(D_MODEL), _const_spec((1, D_MODEL)), _const_spec((D_MODEL, _N_MAIN)),
                  _const_spec((1, _N_MAIN)), _const_spec((8, D_MODEL)), _const_spec((8, 1))],
        out_specs=[row(4 * ML_WP), row(SB_W), row(SB_W), row(SB_W),
                   pl.BlockSpec((tm, SSM_W), lambda bi, ti: (ti, bi)),
                   pl.BlockSpec((1, 8, tm), lambda bi, ti: (bi * nt + ti, 0, 0))],
        out_shape=[jax.ShapeDtypeStruct((n, 4 * ML_WP), F32),
                   jax.ShapeDtypeStruct((n, SB_W), F32),
                   jax.ShapeDtypeStruct((n, SB_W), F32),
                   jax.ShapeDtypeStruct((n, SB_W), F32),
                   jax.ShapeDtypeStruct((t_len, bsz * SSM_W), F32),
                   jax.ShapeDtypeStruct((bsz * nt, 8, tm), F32)],
        compiler_params=_cparams("parallel", "parallel"),
        name="in_proj",
    )(x, g, w, b, wg, bg)


def _mlstm_body(blk, q_ref, k_ref, v_ref, o_ref, g_ref, c0_ref, n0_ref, m0_ref, gn_ref,
                ya_ref, c_ref, n_ref, m_ref):
    @pl.when(pl.program_id(1) == 0)
    def _():
        c_ref[...] = c0_ref[...]
        n_ref[...] = n0_ref[...]
        m_ref[...] = m0_ref[...]

    nb, ts = q_ref.shape[0], q_ref.shape[1]
    t_i = lax.broadcasted_iota(jnp.int32, (blk, blk), 0)
    s_i = lax.broadcasted_iota(jnp.int32, (blk, blk), 1)
    causal = s_i <= t_i
    eye = s_i == t_i
    upper = jnp.where(t_i <= s_i, 1.0, 0.0).astype(BF16)
    k_scale = ML_DH ** -0.5

    units = [(bi, h) for bi in range(nb) for h in range(ML_HEADS)]
    each = lambda f: [f(u) for u in range(len(units))]
    for ci in range(ts // blk):
        rows = slice(ci * blk, (ci + 1) * blk)
        cols = lambda u: slice(units[u][1] * ML_DHP, (units[u][1] + 1) * ML_DHP)
        g = [g_ref[bi, 0, :, rows] for bi in range(nb)]
        lf_all = [_log_sigmoid(x) for x in g]
        b_rows = [_split_dot(x, upper, 3) for x in lf_all]
        q = each(lambda u: q_ref[units[u][0], rows, cols(u)])
        k = each(lambda u: k_ref[units[u][0], rows, cols(u)] * k_scale)
        v = each(lambda u: v_ref[units[u][0], rows, cols(u)])
        qb = each(lambda u: q[u].astype(BF16))
        kb = each(lambda u: k[u].astype(BF16))
        vb = each(lambda u: v[u].astype(BF16))
        lf = each(lambda u: lf_all[units[u][0]][4 + units[u][1]:5 + units[u][1], :])
        a_row = each(lambda u: g[units[u][0]][units[u][1]:units[u][1] + 1, :]
                     - b_rows[units[u][0]][4 + units[u][1]:5 + units[u][1], :])
        b_col = each(lambda u: jnp.sum(jnp.where(causal, lf[u], 0.0), axis=-1, keepdims=True))
        m_prev = each(lambda u: m_ref[units[u][0], units[u][1]][:, 0:1])
        n_row = each(lambda u: n_ref[units[u][0], units[u][1]])
        ct = each(lambda u: c_ref[units[u][0], units[u][1]])

        a_m = each(lambda u: jnp.where(causal, a_row[u], -jnp.inf))
        m_col = each(lambda u: jnp.maximum(jnp.max(a_m[u], axis=-1, keepdims=True), m_prev[u]))
        w_intra = each(lambda u: jnp.exp(a_m[u] - m_col[u]))
        w_inter = each(lambda u: jnp.exp(m_prev[u] - m_col[u]))
        qk = each(lambda u: _dot_nt(qb[u], kb[u]))
        qc = each(lambda u: _dot(qb[u], ct[u].astype(BF16)))
        s = each(lambda u: qk[u] * w_intra[u])
        sv = each(lambda u: _dot(s[u].astype(BF16), vb[u]))
        qn = each(lambda u: jnp.sum(q[u] * n_row[u], axis=-1, keepdims=True))
        den = each(lambda u: jnp.sum(s[u], axis=-1, keepdims=True) + w_inter[u] * qn[u])
        num = each(lambda u: sv[u] + w_inter[u] * qc[u])
        hh = each(lambda u: num[u] / jnp.maximum(jnp.abs(den[u]), jnp.exp(-(b_col[u] + m_col[u]))))
        x = each(lambda u: _sigmoid(o_ref[units[u][0], rows, cols(u)]) * hh[u])
        ms = each(lambda u: jnp.sum(x[u] * x[u], axis=-1, keepdims=True) * (1.0 / ML_DH))
        for u, (bi, h) in enumerate(units):
            ya_ref[bi, rows, cols(u)] = x[u] * lax.rsqrt(ms[u] + EPS) * gn_ref[:, cols(u)]

        m_last = each(lambda u: m_col[u][blk - 1:blk, :])
        wg_row = each(lambda u: jnp.exp(a_row[u] - m_last[u]))
        wg_col = each(lambda u: jnp.sum(jnp.where(eye, wg_row[u], 0.0), axis=-1, keepdims=True))
        decay = each(lambda u: jnp.exp(m_prev[u] - m_last[u]))
        kv = each(lambda u: _dot_tn((k[u] * wg_col[u]).astype(BF16), vb[u]))
        kn = each(lambda u: _dot(jnp.broadcast_to(wg_row[u], (8, blk)).astype(BF16), kb[u])[0:1, :])
        for u, (bi, h) in enumerate(units):
            c_ref[bi, h] = decay[u] * ct[u] + kv[u]
            n_ref[bi, h] = decay[u] * n_row[u] + kn[u]
            m_ref[bi, h] = jnp.broadcast_to(b_col[u][blk - 1:blk, :] + m_last[u], (1, LANE))


def _mlstm(za, gates, c0, n0, m0, gn, bsz, t_len, ts, blk, nb):
    ns = t_len // ts
    col = lambda j: pl.BlockSpec((nb, ts, ML_WP), lambda bi, si: (bi, si, j))
    st4 = pl.BlockSpec((nb, ML_HEADS, ML_DHP, ML_DHP), lambda bi, si: (bi, 0, 0, 0))
    st3 = pl.BlockSpec((nb, ML_HEADS, 1, LANE), lambda bi, si: (bi, 0, 0, 0))
    return pl.pallas_call(
        functools.partial(_mlstm_body, blk),
        grid=(bsz // nb, ns),
        in_specs=[col(0), col(1), col(2), col(3),
                  pl.BlockSpec((nb, 1, 8, ts), lambda bi, si: (bi, si, 0, 0)),
                  st4, st3, st3, _const_spec((1, ML_WP))],
        out_specs=[pl.BlockSpec((nb, ts, ML_WP), lambda bi, si: (bi, si, 0)), st4, st3, st3],
        out_shape=[jax.ShapeDtypeStruct((bsz, t_len, ML_WP), F32),
                   jax.ShapeDtypeStruct((bsz, ML_HEADS, ML_DHP, ML_DHP), F32),
                   jax.ShapeDtypeStruct((bsz, ML_HEADS, 1, LANE), F32),
                   jax.ShapeDtypeStruct((bsz, ML_HEADS, 1, LANE), F32)],
        compiler_params=_cparams("parallel", "arbitrary"),
        name="mlstm",
    )(za, za, za, za, gates, c0, n0, m0, gn)


def _sb_body(past_off, n_q, q_ref, kc_ref, vc_ref, kp_ref, vp_ref, gn_ref, y_ref, acc_ref, car_ref):
    tq = q_ref.shape[1]
    qi = pl.program_id(1)
    lane = lax.broadcasted_iota(jnp.int32, (1, LANE), 1)
    first = lane < SB_DH
    q = q_ref[0] * (SB_DH ** -0.5)
    qh = []
    for p in range(SB_PAIRS):
        qp = q[:, p * LANE:(p + 1) * LANE]
        qh += [jnp.where(first, qp, 0.0).astype(BF16), jnp.where(first, 0.0, qp).astype(BF16)]

    def ones_and_after(kb):
        r = lax.broadcasted_iota(jnp.int32, (kb, kb), 0)
        c = lax.broadcasted_iota(jnp.int32, (kb, kb), 1)
        return jnp.where(r >= c, 1.0, 0.0).astype(BF16)

    def block(k, v, uo, vis):
        kb = k.shape[0]
        heads = range(SB_HEADS)
        kbf = [k[:, p * LANE:(p + 1) * LANE].astype(BF16) for p in range(SB_PAIRS)]
        vbf = [v[:, p * LANE:(p + 1) * LANE].astype(BF16) for p in range(SB_PAIRS)]
        z = [_dot_nt(qh[hd], kbf[hd // 2]) for hd in heads]
        l1m = [-(jnp.maximum(z[hd], 0.0) + jnp.log(1.0 + jnp.exp(-jnp.abs(z[hd])))) for hd in heads]
        lmask = l1m if vis is None else [jnp.where(vis, l1m[hd], 0.0) for hd in heads]
        cs = [_split_dot(lmask[hd], uo, 2) for hd in heads]
        rs = [jnp.broadcast_to(cs[hd][:, 0:1], (tq, LANE)) for hd in heads]
        if vis is None:
            car = [car_ref[hd] for hd in heads]
            car = [c[:, :kb] if kb <= LANE else jnp.concatenate([c] * (kb // LANE), axis=1) for c in car]
            a = [jnp.exp(z[hd] + (car[hd] + cs[hd])) for hd in heads]
            pv = [_dot(a[hd].astype(BF16), vbf[hd // 2]) for hd in heads]
            for hd in heads:
                acc_ref[hd] += pv[hd]
                car_ref[hd] += rs[hd]
        else:
            a = [jnp.where(vis, jnp.exp(z[hd] + cs[hd]), 0.0) for hd in heads]
            pv = [_dot(a[hd].astype(BF16), vbf[hd // 2]) for hd in heads]
            for hd in heads:
                acc_ref[hd] = pv[hd]
                car_ref[hd] = rs[hd]

    uo_cur = ones_and_after(tq)
    uo_past = uo_cur if tq == KEY_BLOCK else ones_and_after(KEY_BLOCK)
    t_i = lax.broadcasted_iota(jnp.int32, (tq, tq), 0)
    s_i = lax.broadcasted_iota(jnp.int32, (tq, tq), 1)
    block(kc_ref[0], vc_ref[0], uo_cur, s_i < t_i)

    if n_q == 1:
        for j in reversed(range(past_off // KEY_BLOCK)):
            block(kp_ref[0, j * KEY_BLOCK:(j + 1) * KEY_BLOCK, :], vp_ref[0, j * KEY_BLOCK:(j + 1) * KEY_BLOCK, :],
                  uo_past, None)
    else:
        n_past = (past_off + qi * tq) // KEY_BLOCK

        def past(j, carry):
            start = pl.multiple_of((n_past - 1 - j) * KEY_BLOCK, KEY_BLOCK)
            block(kp_ref[0, pl.ds(start, KEY_BLOCK), :], vp_ref[0, pl.ds(start, KEY_BLOCK), :], uo_past, None)
            return carry

        lax.fori_loop(0, n_past, past, 0)

    for p in range(SB_PAIRS):
        o = jnp.where(first, acc_ref[2 * p], acc_ref[2 * p + 1])
        o2 = o * o
        ss0 = jnp.sum(jnp.where(first, o2, 0.0), axis=-1, keepdims=True)
        ss1 = jnp.sum(jnp.where(first, 0.0, o2), axis=-1, keepdims=True)
        inv = jnp.where(first, lax.rsqrt(ss0 * (1.0 / SB_DH) + EPS), lax.rsqrt(ss1 * (1.0 / SB_DH) + EPS))
        y_ref[0, :, p * LANE:(p + 1) * LANE] = o * inv * gn_ref[:, p * LANE:(p + 1) * LANE]


def _sb_attn(q, kc, vc, kp, vp, gn, tq, past_off):
    bsz, t_len, _ = q.shape
    p_len = kp.shape[1]
    n_q = t_len // tq
    assert (past_off % KEY_BLOCK == 0) and (n_q == 1 or tq % KEY_BLOCK == 0)
    cur = pl.BlockSpec((1, tq, SB_W), lambda bi, qi: (bi, qi, 0))
    past = pl.BlockSpec((1, p_len, SB_W), lambda bi, qi: (bi, 0, 0))
    return pl.pallas_call(
        functools.partial(_sb_body, past_off, n_q),
        grid=(bsz, n_q),
        in_specs=[cur, cur, cur, past, past, _const_spec((1, SB_W))],
        out_specs=cur,
        out_shape=jax.ShapeDtypeStruct((bsz, t_len, SB_W), F32),
        scratch_shapes=[pltpu.VMEM((SB_HEADS, tq, LANE), F32), pltpu.VMEM((SB_HEADS, tq, LANE), F32)],
        compiler_params=_cparams("parallel", "arbitrary"),
        name="sb_attn",
    )(q, kc, vc, kp, vp, gn)


def _s5_prep_body(are_ref, aim_ref, ldt_ref, bre_ref, bim_ref, ab_ref, bcat_ref):
    a_re, a_im = are_ref[...], aim_ref[...]
    dt = jnp.exp(ldt_ref[...])
    mag = jnp.exp(a_re * dt)
    ab_re = mag * jnp.cos(a_im * dt)
    ab_im = mag * jnp.sin(a_im * dt)
    den = a_re * a_re + a_im * a_im
    nr = ab_re - 1.0
    zr = (nr * a_re + ab_im * a_im) / den
    zi = (ab_im * a_re - nr * a_im) / den
    ab_ref[0:1, :] = ab_re
    ab_ref[1:2, :] = ab_im
    b_re, b_im = bre_ref[...], bim_ref[...]
    bcat_ref[:, :SSM_N] = (zr * b_re - zi * b_im).astype(BF16)
    bcat_ref[:, SSM_N:] = (zr * b_im + zi * b_re).astype(BF16)


def _s5_prep(a_re, a_im, log_dt, b_re_blk, b_im_blk):
    return pl.pallas_call(
        _s5_prep_body,
        out_shape=[jax.ShapeDtypeStruct((2, SSM_N), F32), jax.ShapeDtypeStruct((SSM_W, 2 * SSM_N), BF16)],
        name="s5_prep",
    )(a_re, a_im, log_dt, b_re_blk, b_im_blk)


def _gelu_tanh(x):
    return 0.5 * x * (1.0 + jnp.tanh(math.sqrt(2.0 / math.pi) * (x + 0.044715 * (x * x * x))))


def _s5_body(u_ref, bcat_ref, ab_ref, cre_ref, cim_ref, d_ref, wglu_ref, bglu_ref, gn_ref,
             h0r_ref, h0i_ref, y_ref, hr_ref, hi_ref, xs_ref):
    tt, bsz, _ = u_ref.shape

    @pl.when(pl.program_id(0) == 0)
    def _():
        hr_ref[...] = h0r_ref[...]
        hi_ref[...] = h0i_ref[...]

    u2 = u_ref[...].reshape(tt * bsz, SSM_W)
    xs_ref[...] = _dot(u2.astype(BF16), bcat_ref[...])
    ar = jnp.broadcast_to(ab_ref[0:1, :], (bsz, SSM_N))
    ai = jnp.broadcast_to(ab_ref[1:2, :], (bsz, SSM_N))

    def step(t, carry):
        hr, hi = carry
        off = pl.multiple_of(t * bsz, bsz)
        xr = xs_ref[pl.ds(off, bsz), :SSM_N]
        xi = xs_ref[pl.ds(off, bsz), SSM_N:]
        nr = ar * hr - ai * hi + xr
        ni = ar * hi + ai * hr + xi
        xs_ref[pl.ds(off, bsz), :SSM_N] = nr
        xs_ref[pl.ds(off, bsz), SSM_N:] = ni
        return nr, ni

    hr, hi = lax.fori_loop(0, tt, step, (hr_ref[...], hi_ref[...]))
    hr_ref[...] = hr
    hi_ref[...] = hi

    y = (_dot(xs_ref[:, :SSM_N].astype(BF16), cre_ref[...])
         - _dot(xs_ref[:, SSM_N:].astype(BF16), cim_ref[...])
         + d_ref[...] * u2)
    g = _gelu_tanh(y)
    gate = _sigmoid(_dot(g.astype(BF16), wglu_ref[...]) + bglu_ref[...])
    y_ref[...] = _rmsnorm(g * gate, gn_ref[...]).reshape(tt, bsz, SSM_W)


def _s5(u_tm, bcat, ab, c_re, c_im, d, w_glu, b_glu, gn, h0r, h0i, tt):
    t_len, bsz, _ = u_tm.shape
    tile = pl.BlockSpec((tt, bsz, SSM_W), lambda ti: (ti, 0, 0))
    st = pl.BlockSpec((bsz, SSM_N), lambda ti: (0, 0))
    return pl.pallas_call(
        _s5_body,
        grid=(t_len // tt,),
        in_specs=[tile, _const_spec((SSM_W, 2 * SSM_N)), _const_spec((2, SSM_N)),
                  _const_spec((SSM_N, SSM_W)), _const_spec((SSM_N, SSM_W)), _const_spec((1, SSM_W)),
                  _const_spec((SSM_W, SSM_W)), _const_spec((1, SSM_W)), _const_spec((1, SSM_W)), st, st],
        out_specs=[tile, st, st],
        out_shape=[jax.ShapeDtypeStruct((t_len, bsz, SSM_W), F32),
                   jax.ShapeDtypeStruct((bsz, SSM_N), F32),
                   jax.ShapeDtypeStruct((bsz, SSM_N), F32)],
        scratch_shapes=[pltpu.VMEM((tt * bsz, 2 * SSM_N), F32)],
        compiler_params=_cparams("arbitrary"),
        name="s5",
    )(u_tm, bcat, ab, c_re, c_im, d, w_glu, b_glu, gn, h0r, h0i)


def _out_proj_body(x_ref, ya_ref, yb_ref, yc_ref, woa_ref, wob_ref, woc_ref, lnx_ref, wxq_ref,
                   x1_ref, qx_ref):
    x1 = (x_ref[...]
          + _dot(ya_ref[...].astype(BF16), woa_ref[...])
          + _dot(yb_ref[...].astype(BF16), wob_ref[...])
          + _dot(yc_ref[...].astype(BF16), woc_ref[...]))
    x1_ref[...] = x1
    hq = _rmsnorm(x1, lnx_ref[...]).astype(BF16)
    qx_ref[...] = _dot(hq, wxq_ref[...]).astype(BF16)


def _out_proj(x, ya, yb, yc_tm, woa, wob, woc, lnx, wxq, bsz, t_len, tm):
    n = bsz * t_len
    nt = t_len // tm
    row = lambda width: pl.BlockSpec((tm, width), lambda bi, ti: (bi * nt + ti, 0))
    return pl.pallas_call(
        _out_proj_body,
        grid=(bsz, nt),
        in_specs=[row(D_MODEL), row(ML_WP), row(SB_W),
                  pl.BlockSpec((tm, SSM_W), lambda bi, ti: (ti, bi)),
                  _const_spec((ML_WP, D_MODEL)), _const_spec((SB_W, D_MODEL)), _const_spec((SSM_W, D_MODEL)),
                  _const_spec((1, D_MODEL)), _const_spec((D_MODEL, D_MODEL))],
        out_specs=[row(D_MODEL), row(D_MODEL)],
        out_shape=[jax.ShapeDtypeStruct((n, D_MODEL), F32), jax.ShapeDtypeStruct((n, D_MODEL), BF16)],
        compiler_params=_cparams("parallel", "parallel"),
        name="out_proj",
    )(x, ya, yb, yc_tm, woa, wob, woc, lnx, wxq)


def _mem_proj_body(m_ref, g_ref, wk_ref, wv_ref, k_ref, v_ref):
    mn = _rmsnorm(m_ref[...], g_ref[...]).astype(BF16)
    k_ref[...] = _dot(mn, wk_ref[...])
    v_ref[...] = _dot(mn, wv_ref[...])


def _mem_proj(mem, g, wk, wv, tm):
    n = mem.shape[0]
    row = pl.BlockSpec((tm, D_MODEL), lambda i: (i, 0))
    return pl.pallas_call(
        _mem_proj_body,
        grid=(n // tm,),
        in_specs=[row, _const_spec((1, D_MODEL)), _const_spec((D_MODEL, D_MODEL)), _const_spec((D_MODEL, D_MODEL))],
        out_specs=[row, row],
        out_shape=[jax.ShapeDtypeStruct((n, D_MODEL), F32)] * 2,
        compiler_params=_cparams("parallel"),
        name="mem_proj",
    )(mem, g, wk, wv)


def _x_attn_body(x_ref, q_ref, k_ref, v_ref, wo_ref, o_ref):
    heads = range(X_HEADS)
    cols = [slice(h * X_DH, (h + 1) * X_DH) for h in heads]
    kh = [k_ref[0, :, c].astype(BF16) for c in cols]
    vh = [v_ref[0, :, c].astype(BF16) for c in cols]
    s = [_dot_nt(q_ref[:, cols[h]], kh[h]) * (X_DH ** -0.5) for h in heads]
    e = [jnp.exp(s[h] - jnp.max(s[h], axis=-1, keepdims=True)) for h in heads]
    p = [e[h] / jnp.sum(e[h], axis=-1, keepdims=True) for h in heads]
    oh = [_dot(p[h].astype(BF16), vh[h]) for h in heads]
    acc = x_ref[...]
    for h in heads:
        acc = acc + _dot(oh[h].astype(BF16), wo_ref[cols[h], :])
    o_ref[...] = acc


def _x_attn(x1, qx, mem_k, mem_v, wo, bsz, t_len, tq):
    n = bsz * t_len
    nt = t_len // tq
    row = pl.BlockSpec((tq, D_MODEL), lambda bi, ti: (bi * nt + ti, 0))
    mem = pl.BlockSpec((1, N_MEM, D_MODEL), lambda bi, ti: (bi, 0, 0))
    return pl.pallas_call(
        _x_attn_body,
        grid=(bsz, nt),
        in_specs=[row, row, mem, mem, _const_spec((D_MODEL, D_MODEL))],
        out_specs=row,
        out_shape=jax.ShapeDtypeStruct((n, D_MODEL), F32),
        compiler_params=_cparams("parallel", "parallel"),
        name="x_attn",
    )(x1, qx, mem_k, mem_v, wo)


def _conv_ffn_body(final_norm, x_ref, g_ref, wa_ref, wb_ref, cw_ref, cb_ref, wd_ref, prev_ref, gf_ref,
                   o_ref, conv_ref, abuf_ref, acc_ref):
    tm = x_ref.shape[0]

    @pl.when(pl.program_id(1) == 0)
    def _():
        conv_ref[...] = prev_ref[...]

    x = x_ref[...]
    hb = _rmsnorm(x, g_ref[...]).astype(BF16)
    acc_ref[...] = x
    n_chunks = D_FF // FF_CHUNK
    up = lambda c: (_dot(hb, wa_ref[:, c * FF_CHUNK:(c + 1) * FF_CHUNK]),
                    _dot(hb, wb_ref[:, c * FF_CHUNK:(c + 1) * FF_CHUNK]))
    nxt = up(0)
    for c in range(n_chunks):
        c0, c1 = c * FF_CHUNK, (c + 1) * FF_CHUNK
        a, gate_in = nxt
        if c + 1 < n_chunks:
            nxt = up(c + 1)
        abuf_ref[6:8, :] = conv_ref[0, :, c0:c1]
        abuf_ref[8:8 + tm, :] = a
        conv_ref[0, :, c0:c1] = a[tm - 2:tm, :]
        cv = (cb_ref[:, c0:c1]
              + abuf_ref[6:6 + tm, :] * cw_ref[0:1, c0:c1]
              + abuf_ref[7:7 + tm, :] * cw_ref[1:2, c0:c1]
              + a * cw_ref[2:3, c0:c1])
        y = cv * _sigmoid(cv) * gate_in
        acc_ref[...] += _dot(y.astype(BF16), wd_ref[c0:c1, :])
    out = acc_ref[...]
    if final_norm:
        out = _rmsnorm(out, gf_ref[...])
    o_ref[...] = out


def _conv_ffn(x, g, wa, wb, cw, cb, wd, prev, gf, bsz, t_len, tm, final_norm):
    n = bsz * t_len
    nt = t_len // tm
    row = pl.BlockSpec((tm, D_MODEL), lambda bi, ti: (bi * nt + ti, 0))
    conv = pl.BlockSpec((1, CONV_W - 1, D_FF), lambda bi, ti: (bi, 0, 0))
    return pl.pallas_call(
        functools.partial(_conv_ffn_body, final_norm),
        grid=(bsz, nt),
        in_specs=[row, _const_spec((1, D_MODEL)), _const_spec((D_MODEL, D_FF)), _const_spec((D_MODEL, D_FF)),
                  _const_spec((CONV_W, D_FF)), _const_spec((1, D_FF)), _const_spec((D_FF, D_MODEL)),
                  conv, _const_spec((1, D_MODEL))],
        out_specs=[row, conv],
        out_shape=[jax.ShapeDtypeStruct((n, D_MODEL), F32),
                   jax.ShapeDtypeStruct((bsz, CONV_W - 1, D_FF), F32)],
        scratch_shapes=[pltpu.VMEM((tm + 8, FF_CHUNK), F32), pltpu.VMEM((tm, D_MODEL), F32)],
        compiler_params=_cparams("parallel", "arbitrary"),
        name="conv_ffn",
    )(x, g, wa, wb, cw, cb, wd, prev, gf)


def _pad_heads(a, axis):
    shp = a.shape
    a = a.reshape(shp[:axis] + (ML_HEADS, ML_DH) + shp[axis + 1:])
    pad = [(0, 0)] * a.ndim
    pad[axis + 1] = (0, ML_DHP - ML_DH)
    a = jnp.pad(a, pad)
    return a.reshape(shp[:axis] + (ML_WP,) + shp[axis + 1:])


def _block_diag(blocks):
    g, r, c = blocks.shape
    eye = jnp.eye(g, dtype=blocks.dtype)
    return (eye[:, None, :, None] * blocks[:, :, None, :]).reshape(g * r, g * c)


def _pack_layer(p):
    w_in, b_in = p["w_in"], p["b_in"]
    o = 0
    parts_w, parts_b = [], []
    for _ in range(4):
        parts_w.append(_pad_heads(w_in[:, o:o + ML_W], 1))
        parts_b.append(_pad_heads(b_in[o:o + ML_W], 0))
        o += ML_W
    wg = w_in[:, o:o + 2 * ML_HEADS].T
    bg = b_in[o:o + 2 * ML_HEADS][:, None]
    o += 2 * ML_HEADS
    parts_w.append(w_in[:, o:])
    parts_b.append(b_in[o:])
    w_out = p["w_out"]
    return dict(
        ln_mix_g=p["ln_mix_g"][None], w_main=jnp.concatenate(parts_w, axis=1).astype(BF16),
        b_main=jnp.concatenate(parts_b)[None], wg=wg.astype(BF16), bg=bg,
        gn_a=_pad_heads(p["gn_a_g"], 0)[None], gn_b=p["gn_b_g"][None], gn_c=p["gn_c_g"][None],
        a_re=p["ssm_a_re"].reshape(1, SSM_N), a_im=p["ssm_a_im"].reshape(1, SSM_N),
        log_dt=p["ssm_log_dt"].reshape(1, SSM_N),
        b_re_blk=_block_diag(jnp.swapaxes(p["ssm_b_re"], 1, 2)),
        b_im_blk=_block_diag(jnp.swapaxes(p["ssm_b_im"], 1, 2)),
        c_re_blk=_block_diag(jnp.swapaxes(p["ssm_c_re"], 1, 2)).astype(BF16),
        c_im_blk=_block_diag(jnp.swapaxes(p["ssm_c_im"], 1, 2)).astype(BF16),
        ssm_d=p["ssm_d"][None], w_glu=p["w_glu"].astype(BF16), b_glu=p["b_glu"][None],
        w_out_a=_pad_heads(w_out[:ML_W], 0).astype(BF16), w_out_b=w_out[ML_W:ML_W + SB_W].astype(BF16),
        w_out_c=w_out[ML_W + SB_W:].astype(BF16),
        ln_x_g=p["ln_x_g"][None], w_xq=p["w_xq"].astype(BF16), w_xo=p["w_xo"].astype(BF16),
        ln_mem_g=p["ln_mem_g"][None], w_xk=p["w_xk"].astype(BF16), w_xv=p["w_xv"].astype(BF16),
        ln_ffn_g=p["ln_ffn_g"][None], w_ffn_a=p["w_ffn_a"].astype(BF16), w_ffn_b=p["w_ffn_b"].astype(BF16),
        ffn_conv_w=p["ffn_conv_w"], ffn_conv_b=p["ffn_conv_b"][None], w_ffn_down=p["w_ffn_down"].astype(BF16),
    )


def _layer(x, w, ssm, mem_k, mem_v, past_k, past_v, c0, n0, m0, s0_re, s0_im, conv_prev, ln_f, cfg, final_norm):
    bsz, t_len = cfg["bsz"], cfg["t_len"]
    tm, ts, blk = cfg["tm"], cfg["ts"], cfg["blk"]
    ab, bcat = ssm

    za, qb, kb, vb, u_tm, gates = _in_proj(x, w["ln_mix_g"], w["w_main"], w["b_main"], w["wg"], w["bg"], bsz, t_len, tm)
    if tm != ts:
        gates = gates.reshape(-1, 8, tm // ts, ts).transpose(0, 2, 1, 3).reshape(-1, 8, ts)

    ct0 = jnp.pad(jnp.swapaxes(c0, -1, -2), ((0, 0), (0, 0), (0, ML_DHP - ML_DH), (0, ML_DHP - ML_DH)))
    n0p = jnp.pad(n0, ((0, 0), (0, 0), (0, ML_DHP - ML_DH)))[:, :, None, :]
    m0p = jnp.broadcast_to(m0[:, :, None, None], (bsz, ML_HEADS, 1, LANE))
    ya, ct, n_new, m_new = _mlstm(za.reshape(bsz, t_len, 4 * ML_WP), gates.reshape(bsz, t_len // ts, 8, ts),
                                  ct0, n0p, m0p, w["gn_a"], bsz, t_len, ts, blk, cfg["nb_ml"])
    ya = ya.reshape(bsz * t_len, ML_WP)

    q3, k3, v3 = (a.reshape(bsz, t_len, SB_W) for a in (qb, kb, vb))
    if past_k is None:
        yb = _sb_attn(q3, k3, v3, k3, v3, w["gn_b"], cfg["tq_sb"], 0)
    else:
        p_len = past_k.shape[1]
        yb = _sb_attn(q3, k3, v3, past_k.reshape(bsz, p_len, SB_W), past_v.reshape(bsz, p_len, SB_W),
                      w["gn_b"], cfg["tq_sb"], p_len)
    yb = yb.reshape(bsz * t_len, SB_W)

    yc_tm, s_re, s_im = _s5(u_tm.reshape(t_len, bsz, SSM_W), bcat, ab, w["c_re_blk"], w["c_im_blk"], w["ssm_d"],
                            w["w_glu"], w["b_glu"], w["gn_c"], s0_re.reshape(bsz, SSM_N), s0_im.reshape(bsz, SSM_N),
                            cfg["tt"])

    x1, qx = _out_proj(x, ya, yb, yc_tm.reshape(t_len, bsz * SSM_W), w["w_out_a"], w["w_out_b"], w["w_out_c"],
                       w["ln_x_g"], w["w_xq"], bsz, t_len, tm)
    x2 = _x_attn(x1, qx, mem_k.reshape(bsz, N_MEM, D_MODEL), mem_v.reshape(bsz, N_MEM, D_MODEL), w["w_xo"],
                 bsz, t_len, tm)
    x3, conv_new = _conv_ffn(x2, w["ln_ffn_g"], w["w_ffn_a"], w["w_ffn_b"], w["ffn_conv_w"], w["ffn_conv_b"],
                             w["w_ffn_down"], conv_prev, ln_f, bsz, t_len, cfg["tm_ffn"], final_norm)

    states = (kb.reshape(bsz, t_len, SB_HEADS, SB_DH), vb.reshape(bsz, t_len, SB_HEADS, SB_DH),
              jnp.swapaxes(ct[:, :, :ML_DH, :ML_DH], -1, -2), n_new[:, :, 0, :ML_DH], m_new[:, :, 0, 0],
              s_re.reshape(bsz, SSM_GROUPS, SSM_P), s_im.reshape(bsz, SSM_GROUPS, SSM_P), conv_new)
    return x3, states


_PROMPT_CFG = dict(tm=512, ts=128, blk=CHUNK, nb_ml=4, tq_sb=256, tt=64, tm_ffn=512)
_SAMPLE_CFG = dict(tm=32, ts=32, blk=32, nb_ml=4, tq_sb=32, tt=32, tm_ffn=32)


def kernel(x_prompt, x_sample, cache_sb_k, cache_sb_v, state_mlstm_c, state_mlstm_n, state_mlstm_m, state_ssm_re, state_ssm_im, state_ffn_conv, cache_mem_k, cache_mem_v, mem_prompt, ln_mix_g, w_in, b_in, gn_a_g, gn_b_g, gn_c_g, ssm_a_re, ssm_a_im, ssm_log_dt, ssm_b_re, ssm_b_im, ssm_c_re, ssm_c_im, ssm_d, w_glu, b_glu, w_out, ln_x_g, ln_mem_g, w_xq, w_xk, w_xv, w_xo, ln_ffn_g, w_ffn_a, w_ffn_b, ffn_conv_w, ffn_conv_b, w_ffn_down, ln_f_g):
    per_layer = dict(ln_mix_g=ln_mix_g, w_in=w_in, b_in=b_in, gn_a_g=gn_a_g, gn_b_g=gn_b_g, gn_c_g=gn_c_g,
                     ssm_a_re=ssm_a_re, ssm_a_im=ssm_a_im, ssm_log_dt=ssm_log_dt, ssm_b_re=ssm_b_re,
                     ssm_b_im=ssm_b_im, ssm_c_re=ssm_c_re, ssm_c_im=ssm_c_im, ssm_d=ssm_d, w_glu=w_glu,
                     b_glu=b_glu, w_out=w_out, ln_x_g=ln_x_g, ln_mem_g=ln_mem_g, w_xq=w_xq, w_xk=w_xk,
                     w_xv=w_xv, w_xo=w_xo, ln_ffn_g=ln_ffn_g, w_ffn_a=w_ffn_a, w_ffn_b=w_ffn_b,
                     ffn_conv_w=ffn_conv_w, ffn_conv_b=ffn_conv_b, w_ffn_down=w_ffn_down)
    depth = w_in.shape[0]
    bp, t_p, _ = x_prompt.shape
    bs, t_s, _ = x_sample.shape
    n_mem = mem_prompt.shape[1]
    cfg_p = dict(_PROMPT_CFG, bsz=bp, t_len=t_p)
    cfg_s = dict(_SAMPLE_CFG, bsz=bs, t_len=t_s)
    ln_f = ln_f_g[None]

    xp = x_prompt.reshape(bp * t_p, D_MODEL)
    xs = x_sample.reshape(bs * t_s, D_MODEL)
    mem_flat = mem_prompt.reshape(bp * n_mem, D_MODEL)
    p_states, s_states = [], []
    for l in range(depth):
        w = _pack_layer({k: v[l] for k, v in per_layer.items()})
        ssm = _s5_prep(w["a_re"], w["a_im"], w["log_dt"], w["b_re_blk"], w["b_im_blk"])
        last = l == depth - 1

        mem_k, mem_v = _mem_proj(mem_flat, w["ln_mem_g"], w["w_xk"], w["w_xv"], 512)
        mem_k = mem_k.reshape(bp, n_mem, X_HEADS, X_DH)
        mem_v = mem_v.reshape(bp, n_mem, X_HEADS, X_DH)
        xp, st = _layer(
            xp, w, ssm, mem_k, mem_v, None, None,
            jnp.zeros((bp, ML_HEADS, ML_DH, ML_DH), F32), jnp.zeros((bp, ML_HEADS, ML_DH), F32),
            jnp.zeros((bp, ML_HEADS), F32),
            jnp.zeros((bp, SSM_GROUPS, SSM_P), F32), jnp.zeros((bp, SSM_GROUPS, SSM_P), F32),
            jnp.zeros((bp, CONV_W - 1, D_FF), F32), ln_f, cfg_p, last)
        p_states.append(st + (mem_k, mem_v))

        xs, st = _layer(
            xs, w, ssm, cache_mem_k[l], cache_mem_v[l], cache_sb_k[l], cache_sb_v[l],
            state_mlstm_c[l], state_mlstm_n[l], state_mlstm_m[l],
            state_ssm_re[l], state_ssm_im[l], state_ffn_conv[l], ln_f, cfg_s, last)
        s_states.append(st)

    p_out = [jnp.stack(a) for a in zip(*p_states)]
    s_out = [jnp.stack(a) for a in zip(*s_states)]
    return (xp.reshape(bp, t_p, D_MODEL), xs.reshape(bs, t_s, D_MODEL), *p_out, *s_out)
```

```python
import functools
import math

import jax
import jax.numpy as jnp
from jax import lax
from jax.experimental import pallas as pl
from jax.experimental.pallas import tpu as pltpu

F32 = jnp.float32
BF16 = jnp.bfloat16

D_MODEL = 1024
EPS = 1e-6
CHUNK = 64
ML_HEADS = 4
ML_DH = 96
ML_DHP = 128
ML_W = ML_HEADS * ML_DH
ML_WP = ML_HEADS * ML_DHP
SB_DH = 64
SB_HEADS = 6
SB_W = SB_HEADS * SB_DH
SB_PAIRS = SB_HEADS // 2
SSM_W = 256
SSM_CH = 16
SSM_GROUPS = 16
SSM_P = 64
SSM_N = SSM_GROUPS * SSM_P
X_HEADS = 4
X_DH = 256
N_MEM = 256
D_FF = 2816
CONV_W = 3
FF_CHUNK = 256
KEY_BLOCK = 256
LANE = 128
VMEM_LIMIT = 56 * 1024 * 1024

_OFF_QB = 4 * ML_WP
_OFF_KB = _OFF_QB + SB_W
_OFF_VB = _OFF_KB + SB_W
_OFF_UC = _OFF_VB + SB_W
_N_MAIN = _OFF_UC + SSM_W


def _cparams(*sem):
    return pltpu.CompilerParams(dimension_semantics=sem, vmem_limit_bytes=VMEM_LIMIT)


def _const_spec(shape):
    nd = len(shape)
    return pl.BlockSpec(shape, lambda *_: (0,) * nd, pipeline_mode=pl.Buffered(1))


def _rmsnorm(x, g):
    ms = jnp.mean(x * x, axis=-1, keepdims=True)
    return x * lax.rsqrt(ms + EPS) * g


def _sigmoid(x):
    return 1.0 / (1.0 + jnp.exp(-x))


def _log_sigmoid(x):
    return jnp.minimum(x, 0.0) - jnp.log(1.0 + jnp.exp(-jnp.abs(x)))


def _dot(a, b):
    return jnp.dot(a, b, preferred_element_type=F32)


def _dot_nt(a, b):
    return lax.dot_general(a, b, (((1,), (1,)), ((), ())), preferred_element_type=F32)


def _dot_tn(a, b):
    return lax.dot_general(a, b, (((0,), (0,)), ((), ())), preferred_element_type=F32)


def _split_dot(x, m_bf16, terms):
    acc = None
    r = x
    for i in range(terms):
        p = r.astype(BF16)
        d = _dot(p, m_bf16)
        acc = d if acc is None else acc + d
        if i + 1 < terms:
            r = r - p.astype(F32)
    return acc


def _in_proj_body(x_ref, g_ref, w_ref, b_ref, wg_ref, bg_ref,
                  za_ref, qb_ref, kb_ref, vb_ref, uc_ref, gt_ref):
    h = _rmsnorm(x_ref[...], g_ref[...]).astype(BF16)

    def proj(lo, hi):
        return _dot(h, w_ref[:, lo:hi]) + b_ref[:, lo:hi]

    za_ref[...] = proj(0, _OFF_QB)
    qb_ref[...] = proj(_OFF_QB, _OFF_KB)
    kb_ref[...] = proj(_OFF_KB, _OFF_VB)
    vb_ref[...] = proj(_OFF_VB, _OFF_UC)
    uc_ref[...] = proj(_OFF_UC, _N_MAIN)
    gt_ref[0] = _dot_nt(wg_ref[...], h) + bg_ref[...]


def _in_proj(x, g, w, b, wg, bg, bsz, t_len, tm):
    n = bsz * t_len
    nt = t_len // tm
    row = lambda width: pl.BlockSpec((tm, width), lambda bi, ti: (bi * nt + ti, 0))
    return pl.pallas_call(
        _in_proj_body,
        grid=(bsz, nt),
        in_specs=[row(D_MODEL), _const_spec((1, D_MODEL)), _const_spec((D_MODEL, _N_MAIN)),
                  _const_spec((1, _N_MAIN)), _const_spec((8, D_MODEL)), _const_spec((8, 1))],
        out_specs=[row(4 * ML_WP), row(SB_W), row(SB_W), row(SB_W),
                   pl.BlockSpec((tm, SSM_W), lambda bi, ti: (ti, bi)),
                   pl.BlockSpec((1, 8, tm), lambda bi, ti: (bi * nt + ti, 0, 0))],
        out_shape=[jax.ShapeDtypeStruct((n, 4 * ML_WP), F32),
                   jax.ShapeDtypeStruct((n, SB_W), F32),
                   jax.ShapeDtypeStruct((n, SB_W), F32),
                   jax.ShapeDtypeStruct((n, SB_W), F32),
                   jax.ShapeDtypeStruct((t_len, bsz * SSM_W), F32),
                   jax.ShapeDtypeStruct((bsz * nt, 8, tm), F32)],
        compiler_params=_cparams("parallel", "parallel"),
        name="in_proj",
    )(x, g, w, b, wg, bg)


def _mlstm_body(blk, q_ref, k_ref, v_ref, o_ref, g_ref, c0_ref, n0_ref, m0_ref, gn_ref,
                ya_ref, c_ref, n_ref, m_ref):
    @pl.when(pl.program_id(1) == 0)
    def _():
        c_ref[...] = c0_ref[...]
        n_ref[...] = n0_ref[...]
        m_ref[...] = m0_ref[...]

    nb, ts = q_ref.shape[0], q_ref.shape[1]
    t_i = lax.broadcasted_iota(jnp.int32, (blk, blk), 0)
    s_i = lax.broadcasted_iota(jnp.int32, (blk, blk), 1)
    causal = s_i <= t_i
    eye = s_i == t_i
    upper = jnp.where(t_i <= s_i, 1.0, 0.0).astype(BF16)
    ones_b = jnp.ones((blk, ML_DHP), BF16)
    ones_d = jnp.ones((ML_DHP, ML_DHP), BF16)
    k_scale = ML_DH ** -0.5

    units = [(bi, h) for bi in range(nb) for h in range(ML_HEADS)]
    each = lambda f: [f(u) for u in range(len(units))]
    for ci in range(ts // blk):
        rows = slice(ci * blk, (ci + 1) * blk)
        cols = lambda u: slice(units[u][1] * ML_DHP, (units[u][1] + 1) * ML_DHP)
        g = [g_ref[bi, 0, :, rows] for bi in range(nb)]
        lf_all = [_log_sigmoid(x) for x in g]
        b_rows = [_split_dot(x, upper, 3) for x in lf_all]
        q = each(lambda u: q_ref[units[u][0], rows, cols(u)])
        k = each(lambda u: k_ref[units[u][0], rows, cols(u)] * k_scale)
        v = each(lambda u: v_ref[units[u][0], rows, cols(u)])
        qb = each(lambda u: q[u].astype(BF16))
        kb = each(lambda u: k[u].astype(BF16))
        vb = each(lambda u: v[u].astype(BF16))
        lf = each(lambda u: lf_all[units[u][0]][4 + units[u][1]:5 + units[u][1], :])
        a_row = each(lambda u: g[units[u][0]][units[u][1]:units[u][1] + 1, :]
                     - b_rows[units[u][0]][4 + units[u][1]:5 + units[u][1], :])
        b_col = each(lambda u: jnp.sum(jnp.where(causal, lf[u], 0.0), axis=-1, keepdims=True))
        m_prev = each(lambda u: m_ref[units[u][0], units[u][1]][:, 0:1])
        n_row = each(lambda u: n_ref[units[u][0], units[u][1]])
        ct = each(lambda u: c_ref[units[u][0], units[u][1]])

        a_m = each(lambda u: jnp.where(causal, a_row[u], -jnp.inf))
        m_col = each(lambda u: jnp.maximum(jnp.max(a_m[u], axis=-1, keepdims=True), m_prev[u]))
        w_intra = each(lambda u: jnp.exp(a_m[u] - m_col[u]))
        w_inter = each(lambda u: jnp.exp(m_prev[u] - m_col[u]))
        qk = each(lambda u: _dot_nt(qb[u], kb[u]))
        qc = each(lambda u: _dot(qb[u], ct[u].astype(BF16)))
        s = each(lambda u: qk[u] * w_intra[u])
        sv = each(lambda u: _dot(s[u].astype(BF16), vb[u]))
        qn = each(lambda u: _split_dot(q[u] * n_row[u], ones_d, 2))
        den = each(lambda u: _split_dot(s[u], ones_b, 2) + w_inter[u] * qn[u])
        num = each(lambda u: sv[u] + w_inter[u] * qc[u])
        hh = each(lambda u: num[u] / jnp.maximum(jnp.abs(den[u]), jnp.exp(-(b_col[u] + m_col[u]))))
        x = each(lambda u: _sigmoid(o_ref[units[u][0], rows, cols(u)]) * hh[u])
        ms = each(lambda u: _split_dot(x[u] * x[u], ones_d, 2) * (1.0 / ML_DH))
        for u, (bi, h) in enumerate(units):
            ya_ref[bi, rows, cols(u)] = x[u] * lax.rsqrt(ms[u] + EPS) * gn_ref[:, cols(u)]

        m_last = each(lambda u: m_col[u][blk - 1:blk, :])
        wg_row = each(lambda u: jnp.exp(a_row[u] - m_last[u]))
        wg_col = each(lambda u: jnp.sum(jnp.where(eye, wg_row[u], 0.0), axis=-1, keepdims=True))
        decay = each(lambda u: jnp.exp(m_prev[u] - m_last[u]))
        kv = each(lambda u: _dot_tn((k[u] * wg_col[u]).astype(BF16), vb[u]))
        kn = each(lambda u: _dot(jnp.broadcast_to(wg_row[u], (8, blk)).astype(BF16), kb[u])[0:1, :])
        for u, (bi, h) in enumerate(units):
            c_ref[bi, h] = decay[u] * ct[u] + kv[u]
            n_ref[bi, h] = decay[u] * n_row[u] + kn[u]
            m_ref[bi, h] = jnp.broadcast_to(b_col[u][blk - 1:blk, :] + m_last[u], (1, LANE))


def _mlstm(za, gates, c0, n0, m0, gn, bsz, t_len, ts, blk, nb):
    ns = t_len // ts
    col = lambda j: pl.BlockSpec((nb, ts, ML_WP), lambda bi, si: (bi, si, j))
    st4 = pl.BlockSpec((nb, ML_HEADS, ML_DHP, ML_DHP), lambda bi, si: (bi, 0, 0, 0))
    st3 = pl.BlockSpec((nb, ML_HEADS, 1, LANE), lambda bi, si: (bi, 0, 0, 0))
    return pl.pallas_call(
        functools.partial(_mlstm_body, blk),
        grid=(bsz // nb, ns),
        in_specs=[col(0), col(1), col(2), col(3),
                  pl.BlockSpec((nb, 1, 8, ts), lambda bi, si: (bi, si, 0, 0)),
                  st4, st3, st3, _const_spec((1, ML_WP))],
        out_specs=[pl.BlockSpec((nb, ts, ML_WP), lambda bi, si: (bi, si, 0)), st4, st3, st3],
        out_shape=[jax.ShapeDtypeStruct((bsz, t_len, ML_WP), F32),
                   jax.ShapeDtypeStruct((bsz, ML_HEADS, ML_DHP, ML_DHP), F32),
                   jax.ShapeDtypeStruct((bsz, ML_HEADS, 1, LANE), F32),
                   jax.ShapeDtypeStruct((bsz, ML_HEADS, 1, LANE), F32)],
        compiler_params=_cparams("parallel", "arbitrary"),
        name="mlstm",
    )(za, za, za, za, gates, c0, n0, m0, gn)


def _sb_body(past_off, n_q, q_ref, kc_ref, vc_ref, kp_ref, vp_ref, gn_ref, y_ref, acc_ref, car_ref):
    tq = q_ref.shape[1]
    qi = pl.program_id(1)
    lane = lax.broadcasted_iota(jnp.int32, (1, LANE), 1)
    first = lane < SB_DH
    q = q_ref[0] * (SB_DH ** -0.5)
    qh = []
    for p in range(SB_PAIRS):
        qp = q[:, p * LANE:(p + 1) * LANE]
        qh += [jnp.where(first, qp, 0.0).astype(BF16), jnp.where(first, 0.0, qp).astype(BF16)]

    def ones_and_after(kb):
        r = lax.broadcasted_iota(jnp.int32, (kb, kb), 0)
        c = lax.broadcasted_iota(jnp.int32, (kb, kb), 1)
        return jnp.where(r >= c, 1.0, 0.0).astype(BF16)

    def block(k, v, uo, vis):
        kb = k.shape[0]
        heads = range(SB_HEADS)
        kbf = [k[:, p * LANE:(p + 1) * LANE].astype(BF16) for p in range(SB_PAIRS)]
        vbf = [v[:, p * LANE:(p + 1) * LANE].astype(BF16) for p in range(SB_PAIRS)]
        z = [_dot_nt(qh[hd], kbf[hd // 2]) for hd in heads]
        l1m = [-(jnp.maximum(z[hd], 0.0) + jnp.log(1.0 + jnp.exp(-jnp.abs(z[hd])))) for hd in heads]
        lmask = l1m if vis is None else [jnp.where(vis, l1m[hd], 0.0) for hd in heads]
        cs = [_split_dot(lmask[hd], uo, 2) for hd in heads]
        rs = [jnp.broadcast_to(cs[hd][:, 0:1], (tq, LANE)) for hd in heads]
        if vis is None:
            car = [car_ref[hd] for hd in heads]
            car = [c[:, :kb] if kb <= LANE else jnp.concatenate([c] * (kb // LANE), axis=1) for c in car]
            a = [jnp.exp(z[hd] + (car[hd] + cs[hd])) for hd in heads]
            pv = [_dot(a[hd].astype(BF16), vbf[hd // 2]) for hd in heads]
            for hd in heads:
                acc_ref[hd] += pv[hd]
                car_ref[hd] += rs[hd]
        else:
            a = [jnp.where(vis, jnp.exp(z[hd] + cs[hd]), 0.0) for hd in heads]
            pv = [_dot(a[hd].astype(BF16), vbf[hd // 2]) for hd in heads]
            for hd in heads:
                acc_ref[hd] = pv[hd]
                car_ref[hd] = rs[hd]

    uo_cur = ones_and_after(tq)
    uo_past = uo_cur if tq == KEY_BLOCK else ones_and_after(KEY_BLOCK)
    t_i = lax.broadcasted_iota(jnp.int32, (tq, tq), 0)
    s_i = lax.broadcasted_iota(jnp.int32, (tq, tq), 1)
    block(kc_ref[0], vc_ref[0], uo_cur, s_i < t_i)

    if n_q == 1:
        for j in reversed(range(past_off // KEY_BLOCK)):
            block(kp_ref[0, j * KEY_BLOCK:(j + 1) * KEY_BLOCK, :], vp_ref[0, j * KEY_BLOCK:(j + 1) * KEY_BLOCK, :],
                  uo_past, None)
    else:
        n_past = (past_off + qi * tq) // KEY_BLOCK

        def past(j, carry):
            start = pl.multiple_of((n_past - 1 - j) * KEY_BLOCK, KEY_BLOCK)
            block(kp_ref[0, pl.ds(start, KEY_BLOCK), :], vp_ref[0, pl.ds(start, KEY_BLOCK), :], uo_past, None)
            return carry

        lax.fori_loop(0, n_past, past, 0)

    for p in range(SB_PAIRS):
        o = jnp.where(first, acc_ref[2 * p], acc_ref[2 * p + 1])
        o2 = o * o
        ss0 = jnp.sum(jnp.where(first, o2, 0.0), axis=-1, keepdims=True)
        ss1 = jnp.sum(jnp.where(first, 0.0, o2), axis=-1, keepdims=True)
        inv = jnp.where(first, lax.rsqrt(ss0 * (1.0 / SB_DH) + EPS), lax.rsqrt(ss1 * (1.0 / SB_DH) + EPS))
        y_ref[0, :, p * LANE:(p + 1) * LANE] = o * inv * gn_ref[:, p * LANE:(p + 1) * LANE]


def _sb_attn(q, kc, vc, kp, vp, gn, tq, past_off):
    bsz, t_len, _ = q.shape
    p_len = kp.shape[1]
    n_q = t_len // tq
    assert (past_off % KEY_BLOCK == 0) and (n_q == 1 or tq % KEY_BLOCK == 0)
    cur = pl.BlockSpec((1, tq, SB_W), lambda bi, qi: (bi, qi, 0))
    past = pl.BlockSpec((1, p_len, SB_W), lambda bi, qi: (bi, 0, 0))
    return pl.pallas_call(
        functools.partial(_sb_body, past_off, n_q),
        grid=(bsz, n_q),
        in_specs=[cur, cur, cur, past, past, _const_spec((1, SB_W))],
        out_specs=cur,
        out_shape=jax.ShapeDtypeStruct((bsz, t_len, SB_W), F32),
        scratch_shapes=[pltpu.VMEM((SB_HEADS, tq, LANE), F32), pltpu.VMEM((SB_HEADS, tq, LANE), F32)],
        compiler_params=_cparams("parallel", "arbitrary"),
        name="sb_attn",
    )(q, kc, vc, kp, vp, gn)


def _s5_prep_body(are_ref, aim_ref, ldt_ref, bre_ref, bim_ref, ab_ref, bcat_ref):
    a_re, a_im = are_ref[...], aim_ref[...]
    dt = jnp.exp(ldt_ref[...])
    mag = jnp.exp(a_re * dt)
    ab_re = mag * jnp.cos(a_im * dt)
    ab_im = mag * jnp.sin(a_im * dt)
    den = a_re * a_re + a_im * a_im
    nr = ab_re - 1.0
    zr = (nr * a_re + ab_im * a_im) / den
    zi = (ab_im * a_re - nr * a_im) / den
    ab_ref[0:1, :] = ab_re
    ab_ref[1:2, :] = ab_im
    b_re, b_im = bre_ref[...], bim_ref[...]
    bcat_ref[:, :SSM_N] = (zr * b_re - zi * b_im).astype(BF16)
    bcat_ref[:, SSM_N:] = (zr * b_im + zi * b_re).astype(BF16)


def _s5_prep(a_re, a_im, log_dt, b_re_blk, b_im_blk):
    return pl.pallas_call(
        _s5_prep_body,
        out_shape=[jax.ShapeDtypeStruct((2, SSM_N), F32), jax.ShapeDtypeStruct((SSM_W, 2 * SSM_N), BF16)],
        name="s5_prep",
    )(a_re, a_im, log_dt, b_re_blk, b_im_blk)


def _gelu_tanh(x):
    return 0.5 * x * (1.0 + jnp.tanh(math.sqrt(2.0 / math.pi) * (x + 0.044715 * (x * x * x))))


def _s5_body(u_ref, bcat_ref, ab_ref, cre_ref, cim_ref, d_ref, wglu_ref, bglu_ref, gn_ref,
             h0r_ref, h0i_ref, y_ref, hr_ref, hi_ref, xs_ref):
    tt, bsz, _ = u_ref.shape

    @pl.when(pl.program_id(0) == 0)
    def _():
        hr_ref[...] = h0r_ref[...]
        hi_ref[...] = h0i_ref[...]

    u2 = u_ref[...].reshape(tt * bsz, SSM_W)
    xs_ref[...] = _dot(u2.astype(BF16), bcat_ref[...])
    ar = jnp.broadcast_to(ab_ref[0:1, :], (bsz, SSM_N))
    ai = jnp.broadcast_to(ab_ref[1:2, :], (bsz, SSM_N))

    def step(t, carry):
        hr, hi = carry
        off = pl.multiple_of(t * bsz, bsz)
        xr = xs_ref[pl.ds(off, bsz), :SSM_N]
        xi = xs_ref[pl.ds(off, bsz), SSM_N:]
        nr = ar * hr - ai * hi + xr
        ni = ar * hi + ai * hr + xi
        xs_ref[pl.ds(off, bsz), :SSM_N] = nr
        xs_ref[pl.ds(off, bsz), SSM_N:] = ni
        return nr, ni

    hr, hi = lax.fori_loop(0, tt, step, (hr_ref[...], hi_ref[...]))
    hr_ref[...] = hr
    hi_ref[...] = hi

    y = (_dot(xs_ref[:, :SSM_N].astype(BF16), cre_ref[...])
         - _dot(xs_ref[:, SSM_N:].astype(BF16), cim_ref[...])
         + d_ref[...] * u2)
    g = _gelu_tanh(y)
    gate = _sigmoid(_dot(g.astype(BF16), wglu_ref[...]) + bglu_ref[...])
    y_ref[...] = _rmsnorm(g * gate, gn_ref[...]).reshape(tt, bsz, SSM_W)


def _s5(u_tm, bcat, ab, c_re, c_im, d, w_glu, b_glu, gn, h0r, h0i, tt):
    t_len, bsz, _ = u_tm.shape
    tile = pl.BlockSpec((tt, bsz, SSM_W), lambda ti: (ti, 0, 0))
    st = pl.BlockSpec((bsz, SSM_N), lambda ti: (0, 0))
    return pl.pallas_call(
        _s5_body,
        grid=(t_len // tt,),
        in_specs=[tile, _const_spec((SSM_W, 2 * SSM_N)), _const_spec((2, SSM_N)),
                  _const_spec((SSM_N, SSM_W)), _const_spec((SSM_N, SSM_W)), _const_spec((1, SSM_W)),
                  _const_spec((SSM_W, SSM_W)), _const_spec((1, SSM_W)), _const_spec((1, SSM_W)), st, st],
        out_specs=[tile, st, st],
        out_shape=[jax.ShapeDtypeStruct((t_len, bsz, SSM_W), F32),
                   jax.ShapeDtypeStruct((bsz, SSM_N), F32),
                   jax.ShapeDtypeStruct((bsz, SSM_N), F32)],
        scratch_shapes=[pltpu.VMEM((tt * bsz, 2 * SSM_N), F32)],
        compiler_params=_cparams("arbitrary"),
        name="s5",
    )(u_tm, bcat, ab, c_re, c_im, d, w_glu, b_glu, gn, h0r, h0i)


def _out_proj_body(x_ref, ya_ref, yb_ref, yc_ref, woa_ref, wob_ref, woc_ref, lnx_ref, wxq_ref,
                   x1_ref, qx_ref):
    x1 = (x_ref[...]
          + _dot(ya_ref[...].astype(BF16), woa_ref[...])
          + _dot(yb_ref[...].astype(BF16), wob_ref[...])
          + _dot(yc_ref[...].astype(BF16), woc_ref[...]))
    x1_ref[...] = x1
    hq = _rmsnorm(x1, lnx_ref[...]).astype(BF16)
    qx_ref[...] = _dot(hq, wxq_ref[...]).astype(BF16)


def _out_proj(x, ya, yb, yc_tm, woa, wob, woc, lnx, wxq, bsz, t_len, tm):
    n = bsz * t_len
    nt = t_len // tm
    row = lambda width: pl.BlockSpec((tm, width), lambda bi, ti: (bi * nt + ti, 0))
    return pl.pallas_call(
        _out_proj_body,
        grid=(bsz, nt),
        in_specs=[row(D_MODEL), row(ML_WP), row(SB_W),
                  pl.BlockSpec((tm, SSM_W), lambda bi, ti: (ti, bi)),
                  _const_spec((ML_WP, D_MODEL)), _const_spec((SB_W, D_MODEL)), _const_spec((SSM_W, D_MODEL)),
                  _const_spec((1, D_MODEL)), _const_spec((D_MODEL, D_MODEL))],
        out_specs=[row(D_MODEL), row(D_MODEL)],
        out_shape=[jax.ShapeDtypeStruct((n, D_MODEL), F32), jax.ShapeDtypeStruct((n, D_MODEL), BF16)],
        compiler_params=_cparams("parallel", "parallel"),
        name="out_proj",
    )(x, ya, yb, yc_tm, woa, wob, woc, lnx, wxq)


def _mem_proj_body(m_ref, g_ref, wk_ref, wv_ref, k_ref, v_ref):
    mn = _rmsnorm(m_ref[...], g_ref[...]).astype(BF16)
    k_ref[...] = _dot(mn, wk_ref[...])
    v_ref[...] = _dot(mn, wv_ref[...])


def _mem_proj(mem, g, wk, wv, tm):
    n = mem.shape[0]
    row = pl.BlockSpec((tm, D_MODEL), lambda i: (i, 0))
    return pl.pallas_call(
        _mem_proj_body,
        grid=(n // tm,),
        in_specs=[row, _const_spec((1, D_MODEL)), _const_spec((D_MODEL, D_MODEL)), _const_spec((D_MODEL, D_MODEL))],
        out_specs=[row, row],
        out_shape=[jax.ShapeDtypeStruct((n, D_MODEL), F32)] * 2,
        compiler_params=_cparams("parallel"),
        name="mem_proj",
    )(mem, g, wk, wv)


def _x_attn_body(x_ref, q_ref, k_ref, v_ref, wo_ref, o_ref):
    heads = range(X_HEADS)
    cols = [slice(h * X_DH, (h + 1) * X_DH) for h in heads]
    kh = [k_ref[0, :, c].astype(BF16) for c in cols]
    vh = [v_ref[0, :, c].astype(BF16) for c in cols]
    s = [_dot_nt(q_ref[:, cols[h]], kh[h]) * (X_DH ** -0.5) for h in heads]
    e = [jnp.exp(s[h] - jnp.max(s[h], axis=-1, keepdims=True)) for h in heads]
    p = [e[h] / jnp.sum(e[h], axis=-1, keepdims=True) for h in heads]
    oh = [_dot(p[h].astype(BF16), vh[h]) for h in heads]
    acc = x_ref[...]
    for h in heads:
        acc = acc + _dot(oh[h].astype(BF16), wo_ref[cols[h], :])
    o_ref[...] = acc


def _x_attn(x1, qx, mem_k, mem_v, wo, bsz, t_len, tq):
    n = bsz * t_len
    nt = t_len // tq
    row = pl.BlockSpec((tq, D_MODEL), lambda bi, ti: (bi * nt + ti, 0))
    mem = pl.BlockSpec((1, N_MEM, D_MODEL), lambda bi, ti: (bi, 0, 0))
    return pl.pallas_call(
        _x_attn_body,
        grid=(bsz, nt),
        in_specs=[row, row, mem, mem, _const_spec((D_MODEL, D_MODEL))],
        out_specs=row,
        out_shape=jax.ShapeDtypeStruct((n, D_MODEL), F32),
        compiler_params=_cparams("parallel", "parallel"),
        name="x_attn",
    )(x1, qx, mem_k, mem_v, wo)


def _conv_ffn_body(final_norm, x_ref, g_ref, wa_ref, wb_ref, cw_ref, cb_ref, wd_ref, prev_ref, gf_ref,
                   o_ref, conv_ref, abuf_ref, acc_ref):
    tm = x_ref.shape[0]

    @pl.when(pl.program_id(1) == 0)
    def _():
        conv_ref[...] = prev_ref[...]

    x = x_ref[...]
    hb = _rmsnorm(x, g_ref[...]).astype(BF16)
    acc_ref[...] = x
    n_chunks = D_FF // FF_CHUNK
    up = lambda c: (_dot(hb, wa_ref[:, c * FF_CHUNK:(c + 1) * FF_CHUNK]),
                    _dot(hb, wb_ref[:, c * FF_CHUNK:(c + 1) * FF_CHUNK]))
    nxt = up(0)
    for c in range(n_chunks):
        c0, c1 = c * FF_CHUNK, (c + 1) * FF_CHUNK
        a, gate_in = nxt
        if c + 1 < n_chunks:
            nxt = up(c + 1)
        abuf_ref[6:8, :] = conv_ref[0, :, c0:c1]
        abuf_ref[8:8 + tm, :] = a
        conv_ref[0, :, c0:c1] = a[tm - 2:tm, :]
        cv = (cb_ref[:, c0:c1]
              + abuf_ref[6:6 + tm, :] * cw_ref[0:1, c0:c1]
              + abuf_ref[7:7 + tm, :] * cw_ref[1:2, c0:c1]
              + a * cw_ref[2:3, c0:c1])
        y = cv * _sigmoid(cv) * gate_in
        acc_ref[...] += _dot(y.astype(BF16), wd_ref[c0:c1, :])
    out = acc_ref[...]
    if final_norm:
        out = _rmsnorm(out, gf_ref[...])
    o_ref[...] = out


def _conv_ffn(x, g, wa, wb, cw, cb, wd, prev, gf, bsz, t_len, tm, final_norm):
    n = bsz * t_len
    nt = t_len // tm
    row = pl.BlockSpec((tm, D_MODEL), lambda bi, ti: (bi * nt + ti, 0))
    conv = pl.BlockSpec((1, CONV_W - 1, D_FF), lambda bi, ti: (bi, 0, 0))
    return pl.pallas_call(
        functools.partial(_conv_ffn_body, final_norm),
        grid=(bsz, nt),
        in_specs=[row, _const_spec((1, D_MODEL)), _const_spec((D_MODEL, D_FF)), _const_spec((D_MODEL, D_FF)),
                  _const_spec((CONV_W, D_FF)), _const_spec((1, D_FF)), _const_spec((D_FF, D_MODEL)),
                  conv, _const_spec((1, D_MODEL))],
        out_specs=[row, conv],
        out_shape=[jax.ShapeDtypeStruct((n, D_MODEL), F32),
                   jax.ShapeDtypeStruct((bsz, CONV_W - 1, D_FF), F32)],
        scratch_shapes=[pltpu.VMEM((tm + 8, FF_CHUNK), F32), pltpu.VMEM((tm, D_MODEL), F32)],
        compiler_params=_cparams("parallel", "arbitrary"),
        name="conv_ffn",
    )(x, g, wa, wb, cw, cb, wd, prev, gf)


def _pad_heads(a, axis):
    shp = a.shape
    a = a.reshape(shp[:axis] + (ML_HEADS, ML_DH) + shp[axis + 1:])
    pad = [(0, 0)] * a.ndim
    pad[axis + 1] = (0, ML_DHP - ML_DH)
    a = jnp.pad(a, pad)
    return a.reshape(shp[:axis] + (ML_WP,) + shp[axis + 1:])


def _block_diag(blocks):
    g, r, c = blocks.shape
    eye = jnp.eye(g, dtype=blocks.dtype)
    return (eye[:, None, :, None] * blocks[:, :, None, :]).reshape(g * r, g * c)


def _pack_layer(p):
    w_in, b_in = p["w_in"], p["b_in"]
    o = 0
    parts_w, parts_b = [], []
    for _ in range(4):
        parts_w.append(_pad_heads(w_in[:, o:o + ML_W], 1))
        parts_b.append(_pad_heads(b_in[o:o + ML_W], 0))
        o += ML_W
    wg = w_in[:, o:o + 2 * ML_HEADS].T
    bg = b_in[o:o + 2 * ML_HEADS][:, None]
    o += 2 * ML_HEADS
    parts_w.append(w_in[:, o:])
    parts_b.append(b_in[o:])
    w_out = p["w_out"]
    return dict(
        ln_mix_g=p["ln_mix_g"][None], w_main=jnp.concatenate(parts_w, axis=1).astype(BF16),
        b_main=jnp.concatenate(parts_b)[None], wg=wg.astype(BF16), bg=bg,
        gn_a=_pad_heads(p["gn_a_g"], 0)[None], gn_b=p["gn_b_g"][None], gn_c=p["gn_c_g"][None],
        a_re=p["ssm_a_re"].reshape(1, SSM_N), a_im=p["ssm_a_im"].reshape(1, SSM_N),
        log_dt=p["ssm_log_dt"].reshape(1, SSM_N),
        b_re_blk=_block_diag(jnp.swapaxes(p["ssm_b_re"], 1, 2)),
        b_im_blk=_block_diag(jnp.swapaxes(p["ssm_b_im"], 1, 2)),
        c_re_blk=_block_diag(jnp.swapaxes(p["ssm_c_re"], 1, 2)).astype(BF16),
        c_im_blk=_block_diag(jnp.swapaxes(p["ssm_c_im"], 1, 2)).astype(BF16),
        ssm_d=p["ssm_d"][None], w_glu=p["w_glu"].astype(BF16), b_glu=p["b_glu"][None],
        w_out_a=_pad_heads(w_out[:ML_W], 0).astype(BF16), w_out_b=w_out[ML_W:ML_W + SB_W].astype(BF16),
        w_out_c=w_out[ML_W + SB_W:].astype(BF16),
        ln_x_g=p["ln_x_g"][None], w_xq=p["w_xq"].astype(BF16), w_xo=p["w_xo"].astype(BF16),
        ln_mem_g=p["ln_mem_g"][None], w_xk=p["w_xk"].astype(BF16), w_xv=p["w_xv"].astype(BF16),
        ln_ffn_g=p["ln_ffn_g"][None], w_ffn_a=p["w_ffn_a"].astype(BF16), w_ffn_b=p["w_ffn_b"].astype(BF16),
        ffn_conv_w=p["ffn_conv_w"], ffn_conv_b=p["ffn_conv_b"][None], w_ffn_down=p["w_ffn_down"].astype(BF16),
    )


def _layer(x, w, ssm, mem_k, mem_v, past_k, past_v, c0, n0, m0, s0_re, s0_im, conv_prev, ln_f, cfg, final_norm):
    bsz, t_len = cfg["bsz"], cfg["t_len"]
    tm, ts, blk = cfg["tm"], cfg["ts"], cfg["blk"]
    ab, bcat = ssm

    za, qb, kb, vb, u_tm, gates = _in_proj(x, w["ln_mix_g"], w["w_main"], w["b_main"], w["wg"], w["bg"], bsz, t_len, tm)
    if tm != ts:
        gates = gates.reshape(-1, 8, tm // ts, ts).transpose(0, 2, 1, 3).reshape(-1, 8, ts)

    ct0 = jnp.pad(jnp.swapaxes(c0, -1, -2), ((0, 0), (0, 0), (0, ML_DHP - ML_DH), (0, ML_DHP - ML_DH)))
    n0p = jnp.pad(n0, ((0, 0), (0, 0), (0, ML_DHP - ML_DH)))[:, :, None, :]
    m0p = jnp.broadcast_to(m0[:, :, None, None], (bsz, ML_HEADS, 1, LANE))
    ya, ct, n_new, m_new = _mlstm(za.reshape(bsz, t_len, 4 * ML_WP), gates.reshape(bsz, t_len // ts, 8, ts),
                                  ct0, n0p, m0p, w["gn_a"], bsz, t_len, ts, blk, cfg["nb_ml"])
    ya = ya.reshape(bsz * t_len, ML_WP)

    q3, k3, v3 = (a.reshape(bsz, t_len, SB_W) for a in (qb, kb, vb))
    if past_k is None:
        yb = _sb_attn(q3, k3, v3, k3, v3, w["gn_b"], cfg["tq_sb"], 0)
    else:
        p_len = past_k.shape[1]
        yb = _sb_attn(q3, k3, v3, past_k.reshape(bsz, p_len, SB_W), past_v.reshape(bsz, p_len, SB_W),
                      w["gn_b"], cfg["tq_sb"], p_len)
    yb = yb.reshape(bsz * t_len, SB_W)

    yc_tm, s_re, s_im = _s5(u_tm.reshape(t_len, bsz, SSM_W), bcat, ab, w["c_re_blk"], w["c_im_blk"], w["ssm_d"],
                            w["w_glu"], w["b_glu"], w["gn_c"], s0_re.reshape(bsz, SSM_N), s0_im.reshape(bsz, SSM_N),
                            cfg["tt"])

    x1, qx = _out_proj(x, ya, yb, yc_tm.reshape(t_len, bsz * SSM_W), w["w_out_a"], w["w_out_b"], w["w_out_c"],
                       w["ln_x_g"], w["w_xq"], bsz, t_len, tm)
    x2 = _x_attn(x1, qx, mem_k.reshape(bsz, N_MEM, D_MODEL), mem_v.reshape(bsz, N_MEM, D_MODEL), w["w_xo"],
                 bsz, t_len, tm)
    x3, conv_new = _conv_ffn(x2, w["ln_ffn_g"], w["w_ffn_a"], w["w_ffn_b"], w["ffn_conv_w"], w["ffn_conv_b"],
                             w["w_ffn_down"], conv_prev, ln_f, bsz, t_len, cfg["tm_ffn"], final_norm)

    states = (kb.reshape(bsz, t_len, SB_HEADS, SB_DH), vb.reshape(bsz, t_len, SB_HEADS, SB_DH),
              jnp.swapaxes(ct[:, :, :ML_DH, :ML_DH], -1, -2), n_new[:, :, 0, :ML_DH], m_new[:, :, 0, 0],
              s_re.reshape(bsz, SSM_GROUPS, SSM_P), s_im.reshape(bsz, SSM_GROUPS, SSM_P), conv_new)
    return x3, states


_PROMPT_CFG = dict(tm=512, ts=128, blk=CHUNK, nb_ml=4, tq_sb=256, tt=64, tm_ffn=512)
_SAMPLE_CFG = dict(tm=32, ts=32, blk=32, nb_ml=4, tq_sb=32, tt=32, tm_ffn=32)


def kernel(x_prompt, x_sample, cache_sb_k, cache_sb_v, state_mlstm_c, state_mlstm_n, state_mlstm_m, state_ssm_re, state_ssm_im, state_ffn_conv, cache_mem_k, cache_mem_v, mem_prompt, ln_mix_g, w_in, b_in, gn_a_g, gn_b_g, gn_c_g, ssm_a_re, ssm_a_im, ssm_log_dt, ssm_b_re, ssm_b_im, ssm_c_re, ssm_c_im, ssm_d, w_glu, b_glu, w_out, ln_x_g, ln_mem_g, w_xq, w_xk, w_xv, w_xo, ln_ffn_g, w_ffn_a, w_ffn_b, ffn_conv_w, ffn_conv_b, w_ffn_down, ln_f_g):
    per_layer = dict(ln_mix_g=ln_mix_g, w_in=w_in, b_in=b_in, gn_a_g=gn_a_g, gn_b_g=gn_b_g, gn_c_g=gn_c_g,
                     ssm_a_re=ssm_a_re, ssm_a_im=ssm_a_im, ssm_log_dt=ssm_log_dt, ssm_b_re=ssm_b_re,
                     ssm_b_im=ssm_b_im, ssm_c_re=ssm_c_re, ssm_c_im=ssm_c_im, ssm_d=ssm_d, w_glu=w_glu,
                     b_glu=b_glu, w_out=w_out, ln_x_g=ln_x_g, ln_mem_g=ln_mem_g, w_xq=w_xq, w_xk=w_xk,
                     w_xv=w_xv, w_xo=w_xo, ln_ffn_g=ln_ffn_g, w_ffn_a=w_ffn_a, w_ffn_b=w_ffn_b,
                     ffn_conv_w=ffn_conv_w, ffn_conv_b=ffn_conv_b, w_ffn_down=w_ffn_down)
    depth = w_in.shape[0]
    bp, t_p, _ = x_prompt.shape
    bs, t_s, _ = x_sample.shape
    n_mem = mem_prompt.shape[1]
    cfg_p = dict(_PROMPT_CFG, bsz=bp, t_len=t_p)
    cfg_s = dict(_SAMPLE_CFG, bsz=bs, t_len=t_s)
    ln_f = ln_f_g[None]

    xp = x_prompt.reshape(bp * t_p, D_MODEL)
    xs = x_sample.reshape(bs * t_s, D_MODEL)
    mem_flat = mem_prompt.reshape(bp * n_mem, D_MODEL)
    p_states, s_states = [], []
    for l in range(depth):
        w = _pack_layer({k: v[l] for k, v in per_layer.items()})
        ssm = _s5_prep(w["a_re"], w["a_im"], w["log_dt"], w["b_re_blk"], w["b_im_blk"])
        last = l == depth - 1

        mem_k, mem_v = _mem_proj(mem_flat, w["ln_mem_g"], w["w_xk"], w["w_xv"], 512)
        mem_k = mem_k.reshape(bp, n_mem, X_HEADS, X_DH)
        mem_v = mem_v.reshape(bp, n_mem, X_HEADS, X_DH)
        xp, st = _layer(
            xp, w, ssm, mem_k, mem_v, None, None,
            jnp.zeros((bp, ML_HEADS, ML_DH, ML_DH), F32), jnp.zeros((bp, ML_HEADS, ML_DH), F32),
            jnp.zeros((bp, ML_HEADS), F32),
            jnp.zeros((bp, SSM_GROUPS, SSM_P), F32), jnp.zeros((bp, SSM_GROUPS, SSM_P), F32),
            jnp.zeros((bp, CONV_W - 1, D_FF), F32), ln_f, cfg_p, last)
        p_states.append(st + (mem_k, mem_v))

        xs, st = _layer(
            xs, w, ssm, cache_mem_k[l], cache_mem_v[l], cache_sb_k[l], cache_sb_v[l],
            state_mlstm_c[l], state_mlstm_n[l], state_mlstm_m[l],
            state_ssm_re[l], state_ssm_im[l], state_ffn_conv[l], ln_f, cfg_s, last)
        s_states.append(st)

    p_out = [jnp.stack(a) for a in zip(*p_states)]
    s_out = [jnp.stack(a) for a in zip(*s_states)]
    return (xp.reshape(bp, t_p, D_MODEL), xs.reshape(bs, t_s, D_MODEL), *p_out, *s_out)
```
